```python
import math
import jax, jax.numpy as jnp
from jax import lax
import numpy as np

D_MODEL = 1024
BATCH = 2
SEQ = 8192
DEPTH = 2
DEC_BATCH = 32
DEC_SEQ = 8
PAST_LEN = 16384
PAGE_SIZE = 128

MIX_W = D_MODEL // 2
DA = 64
DKA = 2 * DA
DVA = 2 * DA
H_A = MIX_W // DVA
DKB = 64
DVB = 128
H_B = MIX_W // DVB
RET_CHUNK = 128
NC = 64
H_C = MIX_W // NC
W_LORA = 64
A_LORA = 64
G_LORA = 128
GN_EPS_C = 64e-5
DD = 128
H_D = MIX_W // DD
N_BUCKETS = 32
MAX_DISTANCE = 128
Q_BLOCK = 128
N_EXPERTS = 16
N_GROUPS = 4
EXPERTS_PER_GROUP = N_EXPERTS // N_GROUPS
TOP_K = 2
D_EXPERT = D_MODEL // 4
DN_ALPHA = (2 * DEPTH) ** 0.25
DN_BETA = (8 * DEPTH) ** -0.25
LN_EPS = 1e-5
RMS_EPS = 1e-5
N_EVEN = (DEPTH + 1) // 2
N_ODD = DEPTH // 2

EVEN_SIZES = (H_A * DKA, H_A * DKA, H_A * DVA, H_B * DKB, H_B * DKB, H_B * DVB, H_B * DVB)
EVEN_IN = sum(EVEN_SIZES)
EVEN_OUT = H_A * DVA + H_B * DVB
C_W = H_C * NC
C_SIZES = (C_W, W_LORA, C_W, C_W, A_LORA, G_LORA)
C_PROJ = sum(C_SIZES)
D_W = H_D * DD
D_SIZES = (D_W, D_W, D_W, H_D)
ODD_IN = C_PROJ + sum(D_SIZES)
ODD_OUT = C_W + D_W

kernel_name = 'hybrid_diffattn_retnet_rwkv7_fox_moe_step'


def split_cols(h, sizes):
    offs = np.cumsum((0,) + tuple(sizes))
    return [h[..., int(offs[i]):int(offs[i + 1])] for i in range(len(sizes))]


def layer_norm(x, g, b):
    xf = x.astype(jnp.float32)
    xc = xf - jnp.mean(xf, -1, keepdims=True)
    var = jnp.mean(xc * xc, -1, keepdims=True)
    return (xc * lax.rsqrt(var + LN_EPS) * g.astype(jnp.float32) + b.astype(jnp.float32)).astype(x.dtype)


def head_rms_norm(x):
    xf = x.astype(jnp.float32)
    return xf * lax.rsqrt(jnp.mean(xf * xf, -1, keepdims=True) + RMS_EPS)


def t5_bucket(dist):
    max_exact = N_BUCKETS // 2
    d = jnp.maximum(dist, 0)
    log_ratio = jnp.log(jnp.maximum(d, 1).astype(jnp.float32) / max_exact) / math.log(MAX_DISTANCE / max_exact)
    large = jnp.minimum(max_exact + (log_ratio * (N_BUCKETS - max_exact)).astype(jnp.int32), N_BUCKETS - 1)
    return jnp.where(d < max_exact, d, large)


def rotary(x, pos):
    half = x.shape[-1] // 2
    inv = 1.0 / (10000.0 ** (jnp.arange(half, dtype=jnp.float32) / half))
    ang = pos.astype(jnp.float32)[:, None] * inv[None, :]
    cos = jnp.cos(ang)[None, :, None, :]
    sin = jnp.sin(ang)[None, :, None, :]
    x1, x2 = x[..., :half], x[..., half:]
    return jnp.concatenate([x1 * cos - x2 * sin, x1 * sin + x2 * cos], -1)


def gather_pages(pool, layer, page_table):
    g = pool[layer, page_table]
    b, n, p = g.shape[:3]
    return g.reshape((b, n * p) + g.shape[3:])


def diff_attend(q, k, v, rel_bias, q_pos, k_pos, lam):
    dist = q_pos[:, None] - k_pos[None, :]
    bias = jnp.moveaxis(rel_bias.astype(jnp.float32)[t5_bucket(dist)], -1, 0)
    s = jnp.einsum('bqhmd,bkhmd->bhmqk', q, k, preferred_element_type=jnp.float32) * (DA ** -0.5)
    s = jnp.where(dist >= 0, s + bias[None, :, None], -jnp.inf)
    p = jax.nn.softmax(s, axis=-1)
    w = p[:, :, 0] - lam * p[:, :, 1]
    return jnp.einsum('bhqk,bkhd->bqhd', w.astype(v.dtype), v)


def diff_attn_prompt(q, k, v, rel_bias, lam):
    B, T = q.shape[:2]
    n = T // Q_BLOCK
    qb = jnp.moveaxis(q.reshape(B, n, Q_BLOCK, H_A, 2, DA), 1, 0)
    k_pos = jnp.arange(T)

    def block(args):
        qi, i = args
        return diff_attend(qi, k, v, rel_bias, i * Q_BLOCK + jnp.arange(Q_BLOCK), k_pos, lam)

    o = lax.map(block, (qb, jnp.arange(n)))
    return jnp.moveaxis(o, 0, 1).reshape(B, T, H_A, DVA)


def retention(q, k, v, s0):
    B, T = q.shape[:2]
    C = RET_CHUNK if T % RET_CHUNK == 0 else T
    n = T // C
    log_g = jnp.log(1.0 - jnp.exp2(-5.0 - jnp.arange(H_B, dtype=jnp.float32)))
    i = jnp.arange(C, dtype=jnp.float32)
    diff = i[:, None] - i[None, :]
    dmat = jnp.where(diff >= 0, jnp.exp(log_g[:, None, None] * jnp.maximum(diff, 0.0)), 0.0)
    q_dec = jnp.exp(log_g[None, :] * (i[:, None] + 1.0))
    k_dec = jnp.exp(log_g[None, :] * (C - 1.0 - i[:, None]))
    g_c = jnp.exp(log_g * C)

    def step(s, blk):
        qc, kc, vc = blk
        att = jnp.einsum('bihd,bjhd->bhij', qc, kc) * dmat[None]
        o = jnp.einsum('bhij,bjhe->bihe', att, vc) + jnp.einsum('bihd,bhde->bihe', qc * q_dec[None, :, :, None], s)
        s = s * g_c[None, :, None, None] + jnp.einsum('bjhd,bjhe->bhde', kc * k_dec[None, :, :, None], vc)
        return s, o

    to_blocks = lambda a: jnp.moveaxis(a.reshape((B, n, C) + a.shape[2:]), 1, 0)
    s, o = lax.scan(step, s0, (to_blocks(q), to_blocks(k), to_blocks(v)))
    return jnp.moveaxis(o, 0, 1).reshape(B, T, H_B, DVB), s


def even_mixer(x, layer, e, past, rel_bias, w_in, lam_params, subln_g, w_out):
    B, T, _ = x.shape
    h = jnp.einsum('btd,de->bte', x, w_in)
    qa, ka, va, qb, kb, vb, gb = split_cols(h, EVEN_SIZES)
    qa = qa.reshape(B, T, H_A, 2, DA)
    ka = ka.reshape(B, T, H_A, 2, DA)
    va = va.reshape(B, T, H_A, DVA)
    lp = lam_params.astype(jnp.float32)
    lam_init = 0.8 - 0.6 * math.exp(-0.3 * layer)
    lam = jnp.exp(jnp.sum(lp[0] * lp[1])) - jnp.exp(jnp.sum(lp[2] * lp[3])) + lam_init
    if past is None:
        start = 0
        oa = diff_attn_prompt(qa, ka, va, rel_bias, lam)
        s0 = jnp.zeros((B, H_B, DKB, DVB), jnp.float32)
    else:
        pt = past['page_table']
        start = pt.shape[1] * PAGE_SIZE
        kp = gather_pages(past['cache_a_k'], e, pt).reshape(B, start, H_A, 2, DA)
        vp = gather_pages(past['cache_a_v'], e, pt)
        oa = diff_attend(qa, jnp.concatenate([kp, ka], 1), jnp.concatenate([vp, va], 1), rel_bias,
                         start + jnp.arange(T), jnp.arange(start + T), lam)
        s0 = past['state_b'][e].astype(jnp.float32)
    oa = head_rms_norm(oa) * subln_g.astype(jnp.float32) * (1.0 - lam_init)
    pos = start + jnp.arange(T)
    qr = rotary(qb.reshape(B, T, H_B, DKB).astype(jnp.float32), pos)
    kr = rotary(kb.reshape(B, T, H_B, DKB).astype(jnp.float32), pos) * (DKB ** -0.5)
    ob, s_new = retention(qr, kr, vb.reshape(B, T, H_B, DVB).astype(jnp.float32), s0)
    ob = head_rms_norm(ob).reshape(B, T, H_B * DVB) * jax.nn.silu(gb.astype(jnp.float32))
    mix = jnp.concatenate([oa.reshape(B, T, H_A * DVA), ob], -1).astype(x.dtype)
    y = jnp.einsum('bte,ed->btd', mix, w_out)
    return y, ka.reshape(B, T, H_A, DKA), va, s_new


def rwkv7(pc, shift0, s0, mu, w0, w2, a0, a2, g2, k_k, k_a, r_k, ln_g, ln_b):
    B, T, _ = pc.shape
    f32 = jnp.float32
    pcf = pc.astype(f32)
    prev = jnp.concatenate([shift0[:, None].astype(f32), pcf[:, :-1]], axis=1)
    pm = pcf + (prev - pcf) * mu.astype(f32)
    r, wd, k, v, ad, gd = split_cols(pm, C_SIZES)
    w_log = -jax.nn.softplus(-(w0.astype(f32) + jnp.tanh(wd) @ w2.astype(f32))) - 0.5
    decay = jnp.exp(-jnp.exp(w_log))
    a = jax.nn.sigmoid(a0.astype(f32) + ad @ a2.astype(f32))
    g = jax.nn.sigmoid(gd) @ g2.astype(f32)
    heads = lambda z: z.reshape(B, T, H_C, NC)
    kk = heads(k * k_k.astype(f32))
    kk = kk / jnp.maximum(jnp.sqrt(jnp.sum(kk * kk, -1, keepdims=True)), 1e-12)
    k = k * (1.0 + (a - 1.0) * k_a.astype(f32))
    r_h, w_h, k_h, v_h, a_h = heads(r), heads(decay), heads(k), heads(v), heads(a)
    b_h = kk * a_h

    def step(S, inp):
        rt, wt, kt, vt, kkt, bt = inp
        sa = jnp.einsum('bhvk,bhk->bhv', S, -kkt)
        S = S * wt[:, :, None, :] + sa[..., None] * bt[:, :, None, :] + vt[..., None] * kt[:, :, None, :]
        return S, jnp.einsum('bhvk,bhk->bhv', S, rt)

    tm = lambda z: jnp.moveaxis(z, 1, 0)
    S, o = lax.scan(step, s0, (tm(r_h), tm(w_h), tm(k_h), tm(v_h), tm(kk), tm(b_h)))
    o = tm(o)
    oc = o - jnp.mean(o, -1, keepdims=True)
    o = oc * lax.rsqrt(jnp.mean(oc * oc, -1, keepdims=True) + GN_EPS_C)
    o = o.reshape(B, T, C_W) * ln_g.astype(f32) + ln_b.astype(f32)
    bonus = jnp.sum(r_h * k_h * r_k.astype(f32), -1, keepdims=True) * v_h
    o = (o + bonus.reshape(B, T, C_W)) * g
    return o, S, pc[:, -1]


def fox_attend(q, k, v, rho, kappa, mask):
    s = jnp.einsum('bqhd,bkhd->bhqk', q, k, preferred_element_type=jnp.float32) * (DD ** -0.5)
    s = jnp.where(mask, s + rho[..., :, None] - kappa[..., None, :], -jnp.inf)
    p = jax.nn.softmax(s, axis=-1)
    return jnp.einsum('bhqk,bkhd->bqhd', p.astype(v.dtype), v)


def fox_prompt(q, k, v, logf):
    B, T = q.shape[:2]
    n = T // Q_BLOCK
    c = jnp.cumsum(logf, axis=1).transpose(0, 2, 1)
    qb = jnp.moveaxis(q.reshape(B, n, Q_BLOCK, H_D, DD), 1, 0)
    cb = jnp.moveaxis(c.reshape(B, H_D, n, Q_BLOCK), 2, 0)
    k_pos = jnp.arange(T)

    def block(args):
        qi, ci, i = args
        q_pos = i * Q_BLOCK + jnp.arange(Q_BLOCK)
        return fox_attend(qi, k, v, ci, c, q_pos[:, None] >= k_pos[None, :])

    o = lax.map(block, (qb, cb, jnp.arange(n)))
    return jnp.moveaxis(o, 0, 1).reshape(B, T, H_D * DD)


def fox_cached(q, k, v, logf, past, o):
    B, T = q.shape[:2]
    pt = past['page_table']
    kp = gather_pages(past['cache_d_k'], o, pt)
    vp = gather_pages(past['cache_d_v'], o, pt)
    lfp = gather_pages(past['cache_d_logf'], o, pt).astype(jnp.float32)
    P = kp.shape[1]
    suffix = lax.cumsum(lfp, axis=1, reverse=True) - lfp
    cn = jnp.cumsum(logf, axis=1)
    kappa = jnp.concatenate([-suffix, cn], 1).transpose(0, 2, 1)
    mask = (P + jnp.arange(T))[:, None] >= jnp.arange(P + T)[None, :]
    od = fox_attend(q, jnp.concatenate([kp, k], 1), jnp.concatenate([vp, v], 1), cn.transpose(0, 2, 1), kappa, mask)
    return od.reshape(B, T, H_D * DD)


def odd_mixer(x, o, past, w_in, mu, w0, w2, a0, a2, g2, k_k, k_a, r_k, ln_g, ln_b, bf, w_out):
    B, T, _ = x.shape
    h = jnp.einsum('btd,de->bte', x, w_in)
    pc = h[..., :C_PROJ]
    qd, kd, vd, fd = split_cols(h[..., C_PROJ:], D_SIZES)
    qd = qd.reshape(B, T, H_D, DD)
    kd = kd.reshape(B, T, H_D, DD)
    vd = vd.reshape(B, T, H_D, DD)
    logf = jax.nn.log_sigmoid(fd.astype(jnp.float32) + bf.astype(jnp.float32))
    if past is None:
        shift0 = jnp.zeros((B, C_PROJ), x.dtype)
        s0 = jnp.zeros((B, H_C, NC, NC), jnp.float32)
        od = fox_prompt(qd, kd, vd, logf)
    else:
        shift0 = past['state_c_shift'][o]
        s0 = past['state_c_wkv'][o].astype(jnp.float32)
        od = fox_cached(qd, kd, vd, logf, past, o)
    oc, s_new, shift_new = rwkv7(pc, shift0, s0, mu, w0, w2, a0, a2, g2, k_k, k_a, r_k, ln_g, ln_b)
    mix = jnp.concatenate([oc, od.astype(jnp.float32)], -1).astype(x.dtype)
    y = jnp.einsum('bte,ed->btd', mix, w_out)
    return y, s_new, shift_new, kd, vd, logf


def moe(x, w_router, b_router, w_gate, w_up, w_down):
    B, T, D = x.shape
    xt = x.reshape(B * T, D)
    logits = jnp.einsum('td,de->te', xt, w_router, preferred_element_type=jnp.float32)
    scores = jax.nn.softmax(logits, axis=-1)
    sel = scores + b_router.astype(jnp.float32)
    grp_score = lax.top_k(sel.reshape(-1, N_GROUPS, EXPERTS_PER_GROUP), TOP_K)[0].sum(-1)
    best = jnp.argmax(grp_score, -1)
    in_group = (jnp.arange(N_EXPERTS) // EXPERTS_PER_GROUP)[None, :] == best[:, None]
    _, idx = lax.top_k(jnp.where(in_group, sel, -jnp.inf), TOP_K)
    w = jnp.take_along_axis(scores, idx, -1)
    w = w / jnp.sum(w, -1, keepdims=True)
    gates = jnp.einsum('tk,tke->te', w, jax.nn.one_hot(idx, N_EXPERTS, dtype=jnp.float32))
    hg = jnp.einsum('td,edf->tef', xt, w_gate)
    hu = jnp.einsum('td,edf->tef', xt, w_up)
    act = jax.nn.silu(hg) * hu * gates[..., None].astype(x.dtype)
    y = jnp.einsum('tef,efd->td', act, w_down)
    return y.reshape(B, T, D)


def run_trunk(x, past, p):
    out = {'a_k': [], 'a_v': [], 'b': [], 'c_wkv': [], 'c_shift': [], 'd_k': [], 'd_v': [], 'd_logf': []}
    for l in range(DEPTH):
        if l % 2 == 0:
            e = l // 2
            mix, ka, va, sb = even_mixer(x, l, e, past, p['rel_bias'], p['w_in_even'][e], p['lam_params'][e],
                                         p['subln_g'][e], p['w_out_even'][e])
            out['a_k'].append(ka)
            out['a_v'].append(va)
            out['b'].append(sb)
        else:
            o = l // 2
            mix, sc, shc, kd, vd, lf = odd_mixer(
                x, o, past, p['w_in_odd'][o], p['c_mu'][o], p['c_w0'][o], p['c_w2'][o], p['c_a0'][o],
                p['c_a2'][o], p['c_g2'][o], p['c_kk'][o], p['c_ka'][o], p['c_rk'][o], p['c_ln_g'][o],
                p['c_ln_b'][o], p['d_bf'][o], p['w_out_odd'][o])
            out['c_wkv'].append(sc)
            out['c_shift'].append(shc)
            out['d_k'].append(kd)
            out['d_v'].append(vd)
            out['d_logf'].append(lf)
        x = layer_norm(DN_ALPHA * x + mix, p['ln1_g'][l], p['ln1_b'][l])
        ff = moe(x, p['w_router'], p['b_router'], p['w_gate'][l], p['w_up'][l], p['w_down'][l])
        x = layer_norm(DN_ALPHA * x + ff, p['ln2_g'][l], p['ln2_b'][l])
    return x, {name: jnp.stack(rows) for name, rows in out.items()}


def _col_scale(sizes, value_slots):
    offs = np.cumsum((0,) + tuple(sizes))
    s = np.ones(int(offs[-1]), np.float32)
    for i in value_slots:
        s[int(offs[i]):int(offs[i + 1])] = DN_BETA
    return jnp.asarray(s)


def setup_inputs(seed: int = 0) -> dict:
    key = jax.random.key(seed)
    keys = iter(jax.random.split(key, 64))

    def nrm(shape, scale=1.0):
        return jax.random.normal(next(keys), shape, jnp.float32) * scale

    def uni(shape, lo, hi):
        return jax.random.uniform(next(keys), shape, jnp.float32, lo, hi)

    n_pages = PAST_LEN // PAGE_SIZE
    n_used = DEC_BATCH * n_pages
    n_pool = n_used + max(1, n_used // 4)
    perm = jax.random.permutation(next(keys), n_pool)
    page_table = perm[:n_used].reshape(DEC_BATCH, n_pages).astype(jnp.int32)
    even_scale = _col_scale(EVEN_SIZES, (2, 5))
    odd_scale = _col_scale(C_SIZES + D_SIZES, (3, len(C_SIZES) + 2))
    return {
        'x_prompt': nrm((BATCH, SEQ, D_MODEL)),
        'x_sample': nrm((DEC_BATCH, DEC_SEQ, D_MODEL)),
        'cache_a_k': nrm((N_EVEN, n_pool, PAGE_SIZE, H_A, DKA)),
        'cache_a_v': nrm((N_EVEN, n_pool, PAGE_SIZE, H_A, DVA), DN_BETA),
        'state_b': nrm((N_EVEN, DEC_BATCH, H_B, DKB, DVB), 0.5),
        'state_c_wkv': nrm((N_ODD, DEC_BATCH, H_C, NC, NC), 0.5),
        'state_c_shift': nrm((N_ODD, DEC_BATCH, C_PROJ)),
        'cache_d_k': nrm((N_ODD, n_pool, PAGE_SIZE, H_D, DD)),
        'cache_d_v': nrm((N_ODD, n_pool, PAGE_SIZE, H_D, DD), DN_BETA),
        'cache_d_logf': jax.nn.log_sigmoid(nrm((N_ODD, n_pool, PAGE_SIZE, H_D)) + 2.0),
        'page_table': page_table,
        'rel_bias': nrm((N_BUCKETS, H_A), 0.5),
        'w_in_even': nrm((N_EVEN, D_MODEL, EVEN_IN), D_MODEL ** -0.5) * even_scale,
        'lam_params': nrm((N_EVEN, 4, DA), 0.1),
        'subln_g': 1.0 + nrm((N_EVEN, DVA), 0.05),
        'w_out_even': nrm((N_EVEN, EVEN_OUT, D_MODEL), EVEN_OUT ** -0.5 * DN_BETA),
        'w_in_odd': nrm((N_ODD, D_MODEL, ODD_IN), D_MODEL ** -0.5) * odd_scale,
        'c_mu': uni((N_ODD, C_PROJ), 0.0, 1.0),
        'c_w0': uni((N_ODD, C_W), -6.0, 1.0),
        'c_w2': nrm((N_ODD, W_LORA, C_W), 0.1),
        'c_a0': nrm((N_ODD, C_W), 0.1),
        'c_a2': nrm((N_ODD, A_LORA, C_W), 0.5 * A_LORA ** -0.5),
        'c_g2': nrm((N_ODD, G_LORA, C_W), G_LORA ** -0.5),
        'c_kk': 1.0 + nrm((N_ODD, C_W), 0.1),
        'c_ka': 1.0 + nrm((N_ODD, C_W), 0.1),
        'c_rk': nrm((N_ODD, H_C, NC), 0.1),
        'c_ln_g': 1.0 + nrm((N_ODD, C_W), 0.05),
        'c_ln_b': nrm((N_ODD, C_W), 0.02),
        'd_bf': 2.0 + nrm((N_ODD, H_D), 0.5),
        'w_out_odd': nrm((N_ODD, ODD_OUT, D_MODEL), ODD_OUT ** -0.5 * DN_BETA),
        'w_router': nrm((D_MODEL, N_EXPERTS), D_MODEL ** -0.5),
        'b_router': nrm((N_EXPERTS,), 0.01),
        'w_gate': nrm((DEPTH, N_EXPERTS, D_MODEL, D_EXPERT), D_MODEL ** -0.5),
        'w_up': nrm((DEPTH, N_EXPERTS, D_MODEL, D_EXPERT), D_MODEL ** -0.5),
        'w_down': nrm((DEPTH, N_EXPERTS, D_EXPERT, D_MODEL), D_EXPERT ** -0.5 * DN_BETA),
        'ln1_g': 1.0 + nrm((DEPTH, D_MODEL), 0.05),
        'ln1_b': nrm((DEPTH, D_MODEL), 0.02),
        'ln2_g': 1.0 + nrm((DEPTH, D_MODEL), 0.05),
        'ln2_b': nrm((DEPTH, D_MODEL), 0.02),
    }


def reference(x_prompt, x_sample, cache_a_k, cache_a_v, state_b, state_c_wkv, state_c_shift,
              cache_d_k, cache_d_v, cache_d_logf, page_table, rel_bias, w_in_even, lam_params,
              subln_g, w_out_even, w_in_odd, c_mu, c_w0, c_w2, c_a0, c_a2, c_g2, c_kk, c_ka, c_rk,
              c_ln_g, c_ln_b, d_bf, w_out_odd, w_router, b_router, w_gate, w_up, w_down,
              ln1_g, ln1_b, ln2_g, ln2_b):
    params = dict(rel_bias=rel_bias, w_in_even=w_in_even, lam_params=lam_params, subln_g=subln_g,
                  w_out_even=w_out_even, w_in_odd=w_in_odd, c_mu=c_mu, c_w0=c_w0, c_w2=c_w2, c_a0=c_a0,
                  c_a2=c_a2, c_g2=c_g2, c_kk=c_kk, c_ka=c_ka, c_rk=c_rk, c_ln_g=c_ln_g, c_ln_b=c_ln_b,
                  d_bf=d_bf, w_out_odd=w_out_odd, w_router=w_router, b_router=b_router, w_gate=w_gate,
                  w_up=w_up, w_down=w_down, ln1_g=ln1_g, ln1_b=ln1_b, ln2_g=ln2_g, ln2_b=ln2_b)
    y_prompt, new_p = run_trunk(x_prompt, None, params)
    past = dict(cache_a_k=cache_a_k, cache_a_v=cache_a_v, state_b=state_b, state_c_wkv=state_c_wkv,
                state_c_shift=state_c_shift, cache_d_k=cache_d_k, cache_d_v=cache_d_v,
                cache_d_logf=cache_d_logf, page_table=page_table)
    y_sample, new_s = run_trunk(x_sample, past, params)
    return (y_prompt, y_sample,
            new_p['a_k'], new_p['a_v'], new_s['a_k'], new_s['a_v'],
            new_p['b'], new_s['b'],
            new_p['c_wkv'], new_s['c_wkv'], new_p['c_shift'], new_s['c_shift'],
            new_p['d_k'], new_p['d_v'], new_p['d_logf'], new_s['d_k'], new_s['d_v'], new_s['d_logf'])
```

```python
import functools
import math

import numpy as np
import jax
import jax.numpy as jnp
from jax import lax
from jax.experimental import pallas as pl
from jax.experimental.pallas import tpu as pltpu

F32 = jnp.float32
BF16 = jnp.bfloat16
HIGHEST = lax.Precision.HIGHEST

D_MODEL = 1024
DEPTH = 2
PAGE = 128
MIX_W = D_MODEL // 2
DA = 64
DKA = 2 * DA
DVA = 2 * DA
H_A = MIX_W // DVA
DKB = 64
DVB = 128
H_B = MIX_W // DVB
RET_CHUNK = 128
NC = 64
H_C = MIX_W // NC
W_LORA = 64
A_LORA = 64
G_LORA = 128
GN_EPS_C = 64e-5
DD = 128
H_D = MIX_W // DD
N_BUCKETS = 32
MAX_DISTANCE = 128
N_EXPERTS = 16
N_GROUPS = 4
EPG = N_EXPERTS // N_GROUPS
D_EXPERT = D_MODEL // 4
DN_ALPHA = (2 * DEPTH) ** 0.25
LN_EPS = 1e-5
RMS_EPS = 1e-5
C_W = H_C * NC
C_PROJ = C_W + W_LORA + C_W + C_W + A_LORA + G_LORA
D_W = H_D * DD
EVEN_IN = 2 * H_A * DKA + H_A * DVA + 2 * H_B * DKB + 2 * H_B * DVB
ODD_IN_PAD = C_PROJ + 3 * D_W + 128

LANES = 128
VMEM_LIMIT = 56 * 1024 * 1024
NEG = -1e30
ATT_TQ = 256
PAGES_PER_STEP = 8
SCAN_TC = 8


def _cparams(sem):
    return pltpu.CompilerParams(dimension_semantics=sem, vmem_limit_bytes=VMEM_LIMIT)


def _pick_tile(m, cap, mult=8):
    best = None
    for t in range(mult, min(m, cap) + 1, mult):
        if m % t == 0:
            best = t
    assert best is not None, (m, cap, mult)
    return best


def _layer_norm_rows(z, g, b):
    mu = jnp.mean(z, axis=-1, keepdims=True)
    zc = z - mu
    var = jnp.mean(zc * zc, axis=-1, keepdims=True)
    return zc * lax.rsqrt(var + LN_EPS) * g + b


def _sigmoid(x):
    return 1.0 / (1.0 + jnp.exp(-x))


def _softplus(y):
    return jnp.maximum(y, 0.0) + jnp.log1p(jnp.exp(-jnp.abs(y)))


def _proj_kernel(x_ref, w_ref, o_ref):
    o_ref[...] = jnp.dot(x_ref[...].astype(BF16), w_ref[...], preferred_element_type=F32)


def _proj(x, w_bf):
    m, k = x.shape
    n = w_bf.shape[1]
    tm = _pick_tile(m, 320)
    return pl.pallas_call(
        _proj_kernel,
        grid=(m // tm,),
        in_specs=[pl.BlockSpec((tm, k), lambda i: (i, 0)),
                  pl.BlockSpec((k, n), lambda i: (0, 0))],
        out_specs=pl.BlockSpec((tm, n), lambda i: (i, 0)),
        out_shape=jax.ShapeDtypeStruct((m, n), F32),
        compiler_params=_cparams(("parallel",)),
        name="proj",
    )(x, w_bf)


def _outproj_kernel(a_ref, b_ref, x_ref, w_ref, g_ref, be_ref, o_ref):
    half = a_ref.shape[1]
    y = jnp.dot(a_ref[...].astype(BF16), w_ref[0:half, :], preferred_element_type=F32)
    y = y + jnp.dot(b_ref[...].astype(BF16), w_ref[half:2 * half, :], preferred_element_type=F32)
    z = DN_ALPHA * x_ref[...] + y
    o_ref[...] = _layer_norm_rows(z, g_ref[...], be_ref[...])


def _outproj_ln(mix_l, mix_r, x, w_bf, g, b):
    m, d = x.shape
    half = mix_l.shape[1]
    tm = _pick_tile(m, 640)
    return pl.pallas_call(
        _outproj_kernel,
        grid=(m // tm,),
        in_specs=[pl.BlockSpec((tm, half), lambda i: (i, 0)),
                  pl.BlockSpec((tm, half), lambda i: (i, 0)),
                  pl.BlockSpec((tm, d), lambda i: (i, 0)),
                  pl.BlockSpec((2 * half, d), lambda i: (0, 0)),
                  pl.BlockSpec((1, d), lambda i: (0, 0)),
                  pl.BlockSpec((1, d), lambda i: (0, 0))],
        out_specs=pl.BlockSpec((tm, d), lambda i: (i, 0)),
        out_shape=jax.ShapeDtypeStruct((m, d), F32),
        compiler_params=_cparams(("parallel",)),
        name="outproj_ln",
    )(mix_l, mix_r, x, w_bf, g.reshape(1, d), b.reshape(1, d))


def _router_kernel(x_ref, wr_ref, br_ref, g_ref):
    logits = _nt_dot(wr_ref[...], x_ref[...].astype(BF16))
    mx = jnp.max(logits, axis=0, keepdims=True)
    ex = jnp.exp(logits - mx)
    scores = ex / jnp.sum(ex, axis=0, keepdims=True)
    sel = scores + br_ref[...]
    rows = [sel[e:e + 1, :] for e in range(N_EXPERTS)]
    srow = [scores[e:e + 1, :] for e in range(N_EXPERTS)]
    grp = []
    for gi in range(N_GROUPS):
        a, b, c, d = rows[EPG * gi:EPG * gi + EPG]
        hi1, lo1 = jnp.maximum(a, b), jnp.minimum(a, b)
        hi2, lo2 = jnp.maximum(c, d), jnp.minimum(c, d)
        top1 = jnp.maximum(hi1, hi2)
        top2 = jnp.maximum(jnp.minimum(hi1, hi2), jnp.maximum(lo1, lo2))
        grp.append(top1 + top2)
    best_v = grp[0]
    best_i = jnp.zeros_like(best_v, dtype=jnp.int32)
    for gi in range(1, N_GROUPS):
        better = grp[gi] > best_v
        best_v = jnp.where(better, grp[gi], best_v)
        best_i = jnp.where(better, gi, best_i)
    chosen = []
    for e in range(N_EXPERTS):
        gi = e // EPG
        rank = jnp.zeros_like(best_i)
        for j in range(EPG * gi, EPG * gi + EPG):
            if j == e:
                continue
            beats = (rows[j] >= rows[e]) if j < e else (rows[j] > rows[e])
            rank = rank + beats.astype(jnp.int32)
        chosen.append((best_i == gi) & (rank < 2))
    wsum = jnp.zeros_like(best_v)
    for e in range(N_EXPERTS):
        wsum = wsum + jnp.where(chosen[e], srow[e], 0.0)
    for e in range(N_EXPERTS):
        g_ref[e:e + 1, :] = jnp.where(chosen[e], srow[e] / wsum, 0.0)


def _router(x, w_router, b_router):
    m, d = x.shape
    tm = _pick_tile(m, 1280, LANES)
    gt = pl.pallas_call(
        _router_kernel,
        grid=(m // tm,),
        in_specs=[pl.BlockSpec((tm, d), lambda i: (i, 0)),
                  pl.BlockSpec((N_EXPERTS, d), lambda i: (0, 0)),
                  pl.BlockSpec((N_EXPERTS, 1), lambda i: (0, 0))],
        out_specs=pl.BlockSpec((N_EXPERTS, tm), lambda i: (0, i)),
        out_shape=jax.ShapeDtypeStruct((N_EXPERTS, m), F32),
        compiler_params=_cparams(("parallel",)),
        name="router",
    )(x, w_router.T.astype(BF16), b_router.reshape(N_EXPERTS, 1))
    return gt.T


def _moe_kernel(x_ref, gate_ref, wgu_ref, wd_ref, g_ref, b_ref, o_ref, xb_ref, acc_ref):
    e = pl.program_id(1)

    @pl.when(e == 0)
    def _():
        xb_ref[...] = x_ref[...].astype(BF16)
        acc_ref[...] = jnp.zeros_like(acc_ref)

    h = jnp.dot(xb_ref[...], wgu_ref[0], preferred_element_type=F32)
    gates = gate_ref[...]
    lane = lax.broadcasted_iota(jnp.int32, gates.shape, 1)
    gcol = jnp.sum(jnp.where(lane == e, gates, 0.0), axis=1, keepdims=True)
    hg = h[:, :D_EXPERT]
    hu = h[:, D_EXPERT:]
    act = hg * _sigmoid(hg) * hu * gcol
    acc_ref[...] += jnp.dot(act.astype(BF16), wd_ref[0], preferred_element_type=F32)

    @pl.when(e == N_EXPERTS - 1)
    def _():
        z = DN_ALPHA * x_ref[...] + acc_ref[...]
        o_ref[...] = _layer_norm_rows(z, g_ref[...], b_ref[...])


def _moe_ln(x, gates, wgu_bf, wd_bf, g, b):
    m, d = x.shape
    tm = _pick_tile(m, 640)
    return pl.pallas_call(
        _moe_kernel,
        grid=(m // tm, N_EXPERTS),
        in_specs=[pl.BlockSpec((tm, d), lambda i, e: (i, 0)),
                  pl.BlockSpec((tm, N_EXPERTS), lambda i, e: (i, 0)),
                  pl.BlockSpec((1, d, 2 * D_EXPERT), lambda i, e: (e, 0, 0)),
                  pl.BlockSpec((1, D_EXPERT, d), lambda i, e: (e, 0, 0)),
                  pl.BlockSpec((1, d), lambda i, e: (0, 0)),
                  pl.BlockSpec((1, d), lambda i, e: (0, 0))],
        out_specs=pl.BlockSpec((tm, d), lambda i, e: (i, 0)),
        out_shape=jax.ShapeDtypeStruct((m, d), F32),
        scratch_shapes=[pltpu.VMEM((tm, d), BF16), pltpu.VMEM((tm, d), F32)],
        compiler_params=_cparams(("parallel", "arbitrary")),
        name="moe_ln",
    )(x, gates, wgu_bf, wd_bf, g.reshape(1, d), b.reshape(1, d))


def _softmax_update(s, v_bf, m_ref, l_ref, acc_ref, idx):
    m_prev = m_ref[idx][:, 0:1]
    l_prev = l_ref[idx][:, 0:1]
    m_new = jnp.maximum(m_prev, jnp.max(s, axis=1, keepdims=True))
    alpha = jnp.exp(m_prev - m_new)
    p = jnp.exp(s - m_new)
    l_new = alpha * l_prev + jnp.sum(p, axis=1, keepdims=True)
    acc_ref[idx] = alpha * acc_ref[idx] + jnp.dot(p.astype(BF16), v_bf, preferred_element_type=F32)
    m_ref[idx] = jnp.broadcast_to(m_new, m_ref.shape[1:])
    l_ref[idx] = jnp.broadcast_to(l_new, l_ref.shape[1:])


def _softmax_init(m_ref, l_ref, acc_ref):
    m_ref[...] = jnp.full(m_ref.shape, NEG, F32)
    l_ref[...] = jnp.zeros(l_ref.shape, F32)
    acc_ref[...] = jnp.zeros(acc_ref.shape, F32)


def _nt_dot(a, b):
    return lax.dot_general(a, b, (((1,), (1,)), ((), ())), preferred_element_type=F32)


def _tn_dot(a, b):
    return lax.dot_general(a, b, (((0,), (0,)), ((), ())), preferred_element_type=F32)


def _t5_bucket_np(dist):
    max_exact = N_BUCKETS // 2
    d = np.maximum(dist, 0)
    ratio = (np.maximum(d, 1).astype(np.float32) / np.float32(max_exact)).astype(np.float32)
    log_ratio = (np.log(ratio).astype(np.float32) / np.float32(math.log(MAX_DISTANCE / max_exact))).astype(np.float32)
    large = np.minimum(max_exact + (log_ratio * np.float32(N_BUCKETS - max_exact)).astype(np.int32), N_BUCKETS - 1)
    return np.where(d < max_exact, d, large).astype(np.int32)


_T5_FAR = int(np.min(np.nonzero(_t5_bucket_np(np.arange(4 * MAX_DISTANCE)) == N_BUCKETS - 1)[0]))
assert np.all(_t5_bucket_np(np.arange(_T5_FAR, 8 * MAX_DISTANCE)) == N_BUCKETS - 1)


def _lam_from_params(lp):
    s01 = jnp.sum(lp[0:1, :] * lp[1:2, :], axis=1, keepdims=True)
    s23 = jnp.sum(lp[2:3, :] * lp[3:4, :], axis=1, keepdims=True)
    return jnp.exp(s01) - jnp.exp(s23)


def _diff_finalize(o0, o1, lam, gain, lam_init):
    o = o0 - lam * o1
    ms = jnp.mean(o * o, axis=-1, keepdims=True)
    return o * lax.rsqrt(ms + RMS_EPS) * gain * (1.0 - lam_init)


def _diff_prompt_kernel(q_ref, k_ref, v_ref, tab_ref, lp_ref, gain_ref, o_ref,
                        kb_ref, vb_ref, m_ref, l_ref, acc_ref, *, tq, lam_init):
    i = pl.program_id(2)

    @pl.when(i == 0)
    def _():
        kb_ref[...] = k_ref[...].astype(BF16)
        vb_ref[...] = v_ref[...].astype(BF16)

    _softmax_init(m_ref, l_ref, acc_ref)
    qf = q_ref[...] * (DA ** -0.5)
    lane = lax.broadcasted_iota(jnp.int32, qf.shape, 1)
    qm = (jnp.where(lane < DA, qf, 0.0).astype(BF16), jnp.where(lane >= DA, qf, 0.0).astype(BF16))

    def chunk(j, bias):
        ks = pl.multiple_of(j * tq, tq)
        kc = kb_ref[pl.ds(ks, tq), :]
        vc = vb_ref[pl.ds(ks, tq), :]
        for mi in range(2):
            s = _nt_dot(qm[mi], kc)
            if bias is not None:
                s = s + bias
            _softmax_update(s, vc, m_ref, l_ref, acc_ref, mi)

    def far_body(j, carry):
        chunk(j, None)
        return carry

    lax.fori_loop(0, jnp.maximum(i - 1, 0), far_body, 0)

    @pl.when(i >= 1)
    def _():
        chunk(i - 1, tab_ref[0, 1])

    chunk(i, tab_ref[0, 0])
    lam = _lam_from_params(lp_ref[...]) + lam_init
    o0 = acc_ref[0] / l_ref[0][:, 0:1]
    o1 = acc_ref[1] / l_ref[1][:, 0:1]
    o_ref[...] = _diff_finalize(o0, o1, lam, gain_ref[...], lam_init)


def _diff_tables(rel_bias, tq):
    qi = np.arange(tq)[:, None]
    kj = np.arange(tq)[None, :]
    rb = rel_bias.astype(F32)
    rel = rb - rb[N_BUCKETS - 1][None, :]
    diag = jnp.where(jnp.asarray(qi >= kj)[..., None], rel[_t5_bucket_np(qi - kj)], NEG)
    prev = rel[_t5_bucket_np(tq + qi - kj)]
    return jnp.moveaxis(jnp.stack([diag, prev], 0), -1, 0)


def _diff_prompt(h0, batch, seq, rel_bias, lam_params, subln_g, lam_init):
    tq = min(ATT_TQ, seq)
    assert seq % tq == 0 and tq + 1 >= _T5_FAR
    nq = seq // tq
    tab = _diff_tables(rel_bias, tq)
    kern = functools.partial(_diff_prompt_kernel, tq=tq, lam_init=lam_init)
    qoff, koff, voff = 0, H_A, 2 * H_A
    return pl.pallas_call(
        kern,
        grid=(batch, H_A, nq),
        in_specs=[pl.BlockSpec((tq, DKA), lambda b, h, i: (b * nq + i, qoff + h)),
                  pl.BlockSpec((seq, DKA), lambda b, h, i: (b, koff + h)),
                  pl.BlockSpec((seq, DVA), lambda b, h, i: (b, voff + h)),
                  pl.BlockSpec((1, 2, tq, tq), lambda b, h, i: (h, 0, 0, 0)),
                  pl.BlockSpec((4, DA), lambda b, h, i: (0, 0)),
                  pl.BlockSpec((1, DVA), lambda b, h, i: (0, 0))],
        out_specs=pl.BlockSpec((tq, DVA), lambda b, h, i: (b * nq + i, h)),
        out_shape=jax.ShapeDtypeStruct((batch * seq, H_A * DVA), F32),
        scratch_shapes=[pltpu.VMEM((seq, DKA), BF16), pltpu.VMEM((seq, DVA), BF16),
                        pltpu.VMEM((2, tq, LANES), F32), pltpu.VMEM((2, tq, LANES), F32),
                        pltpu.VMEM((2, tq, DVA), F32)],
        compiler_params=_cparams(("parallel", "parallel", "arbitrary")),
        name="diff_prompt",
    )(h0, h0, h0, tab, lam_params, subln_g.reshape(1, DVA))


def _fox_prompt_kernel(q_ref, k_ref, v_ref, c_ref, kap_ref, o_ref,
                       kb_ref, vb_ref, m_ref, l_ref, acc_ref, *, tq):
    h = pl.program_id(1)
    i = pl.program_id(2)

    @pl.when(i == 0)
    def _():
        kb_ref[...] = k_ref[...].astype(BF16)
        vb_ref[...] = v_ref[...].astype(BF16)

    _softmax_init(m_ref, l_ref, acc_ref)
    qb = q_ref[...].astype(BF16)
    cblk = c_ref[...]
    lane = lax.broadcasted_iota(jnp.int32, cblk.shape, 1)
    rho = jnp.sum(jnp.where(lane == h, cblk, 0.0), axis=1, keepdims=True)

    def chunk(j, masked):
        ks = pl.multiple_of(j * tq, tq)
        kc = kb_ref[pl.ds(ks, tq), :]
        vc = vb_ref[pl.ds(ks, tq), :]
        s = _nt_dot(qb, kc) * (DD ** -0.5) + (rho - kap_ref[0, :, pl.ds(ks, tq)])
        if masked:
            r = lax.broadcasted_iota(jnp.int32, s.shape, 0)
            c = lax.broadcasted_iota(jnp.int32, s.shape, 1)
            s = jnp.where(r >= c, s, NEG)
        _softmax_update(s, vc, m_ref, l_ref, acc_ref, 0)

    def far_body(j, carry):
        chunk(j, False)
        return carry

    lax.fori_loop(0, i, far_body, 0)
    chunk(i, True)
    o_ref[...] = acc_ref[0] / l_ref[0][:, 0:1]


def _fox_prompt(h1, c_all, batch, seq):
    tq = min(ATT_TQ, seq)
    nq = seq // tq
    qoff = C_PROJ // LANES
    koff = qoff + H_D
    voff = koff + H_D
    kap = c_all[:batch * seq, :H_D].reshape(batch, seq, H_D).transpose(0, 2, 1).reshape(batch * H_D, 1, seq)
    kern = functools.partial(_fox_prompt_kernel, tq=tq)
    return pl.pallas_call(
        kern,
        grid=(batch, H_D, nq),
        in_specs=[pl.BlockSpec((tq, DD), lambda b, h, i: (b * nq + i, qoff + h)),
                  pl.BlockSpec((seq, DD), lambda b, h, i: (b, koff + h)),
                  pl.BlockSpec((seq, DD), lambda b, h, i: (b, voff + h)),
                  pl.BlockSpec((tq, LANES), lambda b, h, i: (b * nq + i, 0)),
                  pl.BlockSpec((1, 1, seq), lambda b, h, i: (b * H_D + h, 0, 0))],
        out_specs=pl.BlockSpec((tq, DD), lambda b, h, i: (b * nq + i, h)),
        out_shape=jax.ShapeDtypeStruct((batch * seq, D_W), F32),
        scratch_shapes=[pltpu.VMEM((seq, DD), BF16), pltpu.VMEM((seq, DD), BF16),
                        pltpu.VMEM((1, tq, LANES), F32), pltpu.VMEM((1, tq, LANES), F32),
                        pltpu.VMEM((1, tq, DD), F32)],
        compiler_params=_cparams(("parallel", "parallel", "arbitrary")),
        name="fox_prompt",
    )(h1, h1, h1, c_all, kap)


def _fox_prep_kernel(fd_ref, bf_ref, tri_ref, lf_ref, c_ref, carry_ref):
    @pl.when(pl.program_id(1) == 0)
    def _():
        carry_ref[...] = jnp.zeros_like(carry_ref)

    x = fd_ref[...] + bf_ref[...]
    lf = -_softplus(-x)
    inc = jnp.dot(tri_ref[...], lf, precision=HIGHEST, preferred_element_type=F32) + carry_ref[0:1, :]
    lf_ref[...] = lf
    c_ref[...] = inc
    tm = inc.shape[0]
    carry_ref[...] = jnp.broadcast_to(inc[tm - 1:tm, :], carry_ref.shape)


def _fox_prep(h1, bf_pad, row0, batch, seq):
    tm = _pick_tile(seq, 256)
    nt = seq // tm
    assert row0 % tm == 0
    blk0 = row0 // tm
    col = (ODD_IN_PAD - LANES) // LANES
    tri = jnp.asarray(np.tril(np.ones((tm, tm), np.float32)))
    return pl.pallas_call(
        _fox_prep_kernel,
        grid=(batch, nt),
        in_specs=[pl.BlockSpec((tm, LANES), lambda b, i: (blk0 + b * nt + i, col)),
                  pl.BlockSpec((1, LANES), lambda b, i: (0, 0)),
                  pl.BlockSpec((tm, tm), lambda b, i: (0, 0))],
        out_specs=[pl.BlockSpec((tm, LANES), lambda b, i: (b * nt + i, 0)),
                   pl.BlockSpec((tm, LANES), lambda b, i: (b * nt + i, 0))],
        out_shape=[jax.ShapeDtypeStruct((batch * seq, LANES), F32),
                   jax.ShapeDtypeStruct((batch * seq, LANES), F32)],
        scratch_shapes=[pltpu.VMEM((8, LANES), F32)],
        compiler_params=_cparams(("parallel", "arbitrary")),
        name="fox_prep",
    )(h1, bf_pad, tri)


def _suffix_kernel(pt_ref, *refs, gpages):
    lf_refs = refs[:gpages]
    ut_ref = refs[gpages]
    o_ref = refs[gpages + 1]
    carry_ref = refs[gpages + 2]

    @pl.when(pl.program_id(1) == 0)
    def _():
        carry_ref[...] = jnp.zeros_like(carry_ref)

    run = carry_ref[:, 0:1]
    for j in reversed(range(gpages)):
        lf = lf_refs[j][...]
        incl = jnp.dot(lf, ut_ref[...], precision=HIGHEST, preferred_element_type=F32)
        o_ref[0, :, j * PAGE:(j + 1) * PAGE] = incl - lf + run
        run = run + incl[:, 0:1]
    carry_ref[...] = jnp.broadcast_to(run, carry_ref.shape)


def _suffix_sums(logf_pool_t, page_table_flat, dbatch, n_pages):
    gp = _pick_tile(n_pages, PAGES_PER_STEP, 1)
    ng = n_pages // gp
    ut = jnp.asarray(np.tril(np.ones((PAGE, PAGE), np.float32)))

    def page_map(j):
        return lambda b, g, pt: (pt[b * n_pages + (ng - 1 - g) * gp + j], 0, 0)

    in_specs = [pl.BlockSpec((None, H_D, PAGE), page_map(j)) for j in range(gp)]
    in_specs.append(pl.BlockSpec((PAGE, PAGE), lambda b, g, pt: (0, 0)))
    return pl.pallas_call(
        functools.partial(_suffix_kernel, gpages=gp),
        grid_spec=pltpu.PrefetchScalarGridSpec(
            num_scalar_prefetch=1,
            grid=(dbatch, ng),
            in_specs=in_specs,
            out_specs=pl.BlockSpec((1, H_D, gp * PAGE), lambda b, g, pt: (b, 0, ng - 1 - g)),
            scratch_shapes=[pltpu.VMEM((H_D, LANES), F32)]),
        out_shape=jax.ShapeDtypeStruct((dbatch, H_D, n_pages * PAGE), F32),
        compiler_params=_cparams(("parallel", "arbitrary")),
        name="fox_suffix",
    )(page_table_flat, *([logf_pool_t] * gp), ut)


def _expand_rows(x, reps):
    return jnp.concatenate([jnp.broadcast_to(x[h:h + 1, :], (reps, x.shape[1])) for h in range(x.shape[0])], axis=0)


def _paged_kernel(pt_ref, *refs, gpages, tdec, mode, scale, lam_init, chunk):
    wq_ref = refs[0]
    k_refs = refs[1:1 + gpages]
    v_refs = refs[1 + gpages:1 + 2 * gpages]
    rest = refs[1 + 2 * gpages:]
    if mode == "diff":
        tab_ref, tself_ref, kn_ref, vn_ref, lp_ref, gain_ref, o_ref, s_ref, vb_ref = rest
    else:
        kap_ref, rho_ref, cnt_ref, kn_ref, vn_ref, o_ref, s_ref, vb_ref = rest
    g = pl.program_id(1)
    ng = pl.num_programs(1)
    gw = gpages * PAGE
    past = s_ref.shape[1]

    wq = wq_ref[0]
    kcat = jnp.concatenate([r[...].astype(BF16) for r in k_refs], axis=0)
    s = _nt_dot(wq, kcat)
    if scale != 1.0:
        s = s * scale
    if mode == "diff":
        s = s + tab_ref[0]
    else:
        s = (s + rho_ref[0]) + _expand_rows(kap_ref[0], tdec)
    off = pl.multiple_of(g * gw, gw)
    s_ref[:, pl.ds(off, gw)] = s
    for j in range(gpages):
        vb_ref[pl.ds(off + j * PAGE, PAGE), :] = v_refs[j][...].astype(BF16)

    @pl.when(g == ng - 1)
    def _():
        kn = kn_ref[0].astype(BF16)
        vn = vn_ref[0].astype(BF16)
        tail = _nt_dot(wq, kn)
        if scale != 1.0:
            tail = tail * scale
        if mode == "diff":
            tail = tail + tself_ref[...]
            lam = _lam_from_params(lp_ref[...]) + lam_init
        else:
            tail = (tail + rho_ref[0]) - _expand_rows(cnt_ref[0], tdec)
            r = lax.broadcasted_iota(jnp.int32, tail.shape, 0)
            c = lax.broadcasted_iota(jnp.int32, tail.shape, 1)
            tail = jnp.where(c <= r % tdec, tail, NEG)
        nchunk = past // chunk

        def sc(ci):
            return s_ref[:, pl.ds(pl.multiple_of(ci * chunk, chunk), chunk)]

        m = lax.fori_loop(0, nchunk, lambda ci, mm: jnp.maximum(mm, jnp.max(sc(ci), axis=1, keepdims=True)),
                          jnp.max(tail, axis=1, keepdims=True))
        den = lax.fori_loop(0, nchunk, lambda ci, dd: dd + jnp.sum(jnp.exp(sc(ci) - m), axis=1, keepdims=True),
                            jnp.sum(jnp.exp(tail - m), axis=1, keepdims=True))

        def weights(sv):
            p = jnp.exp(sv - m) / den
            if mode == "diff":
                p = jnp.concatenate([p[2 * h * tdec:(2 * h + 1) * tdec] - lam * p[(2 * h + 1) * tdec:(2 * h + 2) * tdec]
                                     for h in range(H_A)], axis=0)
            return p.astype(BF16)

        def pv_body(ci, acc):
            vc = vb_ref[pl.ds(pl.multiple_of(ci * chunk, chunk), chunk), :]
            return acc + jnp.dot(weights(sc(ci)), vc, preferred_element_type=F32)

        o_all = lax.fori_loop(0, nchunk, pv_body, jnp.dot(weights(tail), vn, preferred_element_type=F32))
        if mode == "diff":
            for h in range(H_A):
                oh = o_all[h * tdec:(h + 1) * tdec, h * DVA:(h + 1) * DVA]
                ms = jnp.mean(oh * oh, axis=-1, keepdims=True)
                o_ref[0, :, h * DVA:(h + 1) * DVA] = oh * lax.rsqrt(ms + RMS_EPS) * gain_ref[...] * (1.0 - lam_init)
        else:
            for h in range(H_D):
                o_ref[0, :, h * DD:(h + 1) * DD] = o_all[h * tdec:(h + 1) * tdec, h * DD:(h + 1) * DD]


def _block_diag_queries(q, n_maps):
    db, t, nh, w = q.shape
    dm = w // n_maps
    g = nh * n_maps
    qg = q.reshape(db, t, g, dm).transpose(0, 2, 1, 3)
    eye = jnp.eye(g, dtype=q.dtype)
    w4 = qg[:, :, :, None, :] * eye[None, :, None, :, None]
    return w4.reshape(db, g * t, g * dm)


def _paged_attention(mode, wq_bf, k_pool, v_pool, page_table_flat, n_pages, k_new, v_new, extra, scale, lam_init=0.0):
    dbatch, rows, width = wq_bf.shape
    tdec = k_new.shape[1]
    gp = _pick_tile(n_pages, PAGES_PER_STEP, 1)
    ng = n_pages // gp
    past = n_pages * PAGE
    chunk = _pick_tile(past, 2048, LANES)
    pad_rows = lambda x: jnp.pad(x, ((0, 0), (0, PAGE - tdec), (0, 0)))
    k_new = pad_rows(k_new)
    v_new = pad_rows(v_new)

    def page_map(j):
        return lambda b, g, pt: (pt[b * n_pages + g * gp + j], 0, 0)

    in_specs = [pl.BlockSpec((1, rows, width), lambda b, g, pt: (b, 0, 0))]
    in_specs += [pl.BlockSpec((None, PAGE, width), page_map(j)) for j in range(gp)]
    in_specs += [pl.BlockSpec((None, PAGE, width), page_map(j)) for j in range(gp)]
    new_spec = pl.BlockSpec((1, PAGE, width), lambda b, g, pt: (b, 0, 0))
    if mode == "diff":
        tab, tself, lam_params, gain = extra
        tself = jnp.pad(tself, ((0, 0), (0, PAGE - tdec)), constant_values=NEG)
        in_specs += [pl.BlockSpec((1, rows, gp * PAGE), lambda b, g, pt: ((g + 1) // ng, 0, 0)),
                     pl.BlockSpec((rows, PAGE), lambda b, g, pt: (0, 0)),
                     new_spec, new_spec,
                     pl.BlockSpec((4, DA), lambda b, g, pt: (0, 0)),
                     pl.BlockSpec((1, DVA), lambda b, g, pt: (0, 0))]
        operands = [tab, tself, k_new, v_new, lam_params, gain]
    else:
        kap, rho, cnt = extra
        cnt = jnp.pad(cnt, ((0, 0), (0, 0), (0, PAGE - tdec)))
        in_specs += [pl.BlockSpec((1, H_D, gp * PAGE), lambda b, g, pt: (b, 0, g)),
                     pl.BlockSpec((1, rows, 1), lambda b, g, pt: (b, 0, 0)),
                     pl.BlockSpec((1, H_D, PAGE), lambda b, g, pt: (b, 0, 0)),
                     new_spec, new_spec]
        operands = [kap, rho, cnt, k_new, v_new]
    kern = functools.partial(_paged_kernel, gpages=gp, tdec=tdec, mode=mode, scale=scale, lam_init=lam_init,
                             chunk=chunk)
    return pl.pallas_call(
        kern,
        grid_spec=pltpu.PrefetchScalarGridSpec(
            num_scalar_prefetch=1,
            grid=(dbatch, ng),
            in_specs=in_specs,
            out_specs=pl.BlockSpec((1, tdec, width), lambda b, g, pt: (b, 0, 0)),
            scratch_shapes=[pltpu.VMEM((rows, past), F32), pltpu.VMEM((past, width), BF16)]),
        out_shape=jax.ShapeDtypeStruct((dbatch, tdec, width), F32),
        compiler_params=_cparams(("parallel", "arbitrary")),
        name="paged_" + mode,
    )(page_table_flat, wq_bf, *([k_pool] * gp), *([v_pool] * gp), *operands)


def _diff_decode_tables(rel_bias, n_pages, gp, tdec):
    rb = rel_bias.astype(F32)
    rel = (rb - rb[N_BUCKETS - 1][None, :]).T
    past = n_pages * PAGE
    t = np.arange(tdec)[:, None]
    kpos = past - gp * PAGE + np.arange(gp * PAGE)[None, :]
    assert PAGE + 1 >= _T5_FAR
    last = rel[:, _t5_bucket_np(past + t - kpos)]
    s = np.arange(tdec)[None, :]
    selfb = jnp.where(jnp.asarray(t >= s)[None], rel[:, _t5_bucket_np(t - s)], NEG)
    rep = lambda x: jnp.broadcast_to(x[:, None], (H_A, 2) + x.shape[1:]).reshape((H_A * 2 * tdec,) + x.shape[2:])
    last_r = rep(last)
    tab = jnp.stack([jnp.zeros_like(last_r), last_r], 0)
    return tab, rep(selfb)


def _retention_kernel(q_ref, k_ref, v_ref, gb_ref, cos_ref, sin_ref, qdec_ref, kdec_ref, dmat_ref,
                      gc_ref, bd_ref, s0_ref, o_ref, sout_ref, s_ref):
    c = pl.program_id(1)

    @pl.when(c == 0)
    def _():
        s_ref[...] = s0_ref[0]

    cos = cos_ref[...]
    sin = sin_ref[...]
    width = cos.shape[1]
    lane = lax.broadcasted_iota(jnp.int32, cos.shape, 1)
    first_half = (lane % DKB) < (DKB // 2)

    def rot(x):
        swapped = jnp.where(first_half, -pltpu.roll(x, width - DKB // 2, 1), pltpu.roll(x, DKB // 2, 1))
        return x * cos + swapped * sin

    qr = rot(q_ref[...])
    kr = rot(k_ref[...]) * (DKB ** -0.5)
    v = v_ref[...]
    vb = v.astype(BF16)
    state = s_ref[...]
    cross = jnp.dot((qr * qdec_ref[...]).astype(BF16), state.astype(BF16), preferred_element_type=F32)
    lane_pair = lax.broadcasted_iota(jnp.int32, (qr.shape[0], LANES), 1)
    gate = gb_ref[...]
    for h in range(H_B):
        cb = (h * DKB) // LANES
        lo = (h * DKB) % LANES
        qh = qr[:, cb * LANES:(cb + 1) * LANES]
        qh = jnp.where((lane_pair >= lo) & (lane_pair < lo + DKB), qh, 0.0).astype(BF16)
        kh = kr[:, cb * LANES:(cb + 1) * LANES].astype(BF16)
        att = _nt_dot(qh, kh) * dmat_ref[h]
        oh = jnp.dot(att.astype(BF16), vb[:, h * DVB:(h + 1) * DVB], preferred_element_type=F32)
        oh = oh + cross[:, h * DVB:(h + 1) * DVB]
        ms = jnp.mean(oh * oh, axis=-1, keepdims=True)
        gh = gate[:, h * DVB:(h + 1) * DVB]
        o_ref[:, h * DVB:(h + 1) * DVB] = oh * lax.rsqrt(ms + RMS_EPS) * (gh * _sigmoid(gh))
    upd = _tn_dot((kr * kdec_ref[...]).astype(BF16), vb)
    new_state = state * gc_ref[...] + upd * bd_ref[...]
    s_ref[...] = new_state

    @pl.when(c == pl.num_programs(1) - 1)
    def _():
        sout_ref[0] = new_state


def _retention(h0, row0, batch, seq, start, s0_bd):
    ch = RET_CHUNK if seq % RET_CHUNK == 0 else seq
    n = seq // ch
    assert row0 % ch == 0
    blk0 = row0 // ch
    kw = H_B * DKB
    vw = H_B * DVB
    qcol = (2 * H_A * DKA + H_A * DVA) // kw
    kcol = qcol + 1
    vcol = (2 * H_A * DKA + H_A * DVA + 2 * kw) // vw
    gcol = vcol + 1
    half = DKB // 2
    inv = 1.0 / (10000.0 ** (jnp.arange(half, dtype=F32) / half))
    pos = (start + jnp.arange(seq)).astype(F32)
    ang = pos[:, None] * inv[None, :]
    cos = jnp.tile(jnp.cos(ang), (1, 2 * H_B))
    sin = jnp.tile(jnp.sin(ang), (1, 2 * H_B))
    log_g = jnp.log(1.0 - jnp.exp2(-5.0 - jnp.arange(H_B, dtype=F32)))
    idx = jnp.arange(ch, dtype=F32)
    diff = idx[:, None] - idx[None, :]
    dmat = jnp.where(diff >= 0, jnp.exp(log_g[:, None, None] * jnp.maximum(diff, 0.0)), 0.0)
    qdec = jnp.repeat(jnp.exp(log_g[None, :] * (idx[:, None] + 1.0)), DKB, axis=1)
    kdec = jnp.repeat(jnp.exp(log_g[None, :] * (ch - 1.0 - idx[:, None])), DKB, axis=1)
    gc = jnp.broadcast_to(jnp.repeat(jnp.exp(log_g * ch), DKB)[:, None], (kw, vw))
    bd = jnp.asarray(np.kron(np.eye(H_B, dtype=np.float32), np.ones((DKB, DVB), np.float32)))
    out, s_fin = pl.pallas_call(
        _retention_kernel,
        grid=(batch, n),
        in_specs=[pl.BlockSpec((ch, kw), lambda b, c: (blk0 + b * n + c, qcol)),
                  pl.BlockSpec((ch, kw), lambda b, c: (blk0 + b * n + c, kcol)),
                  pl.BlockSpec((ch, vw), lambda b, c: (blk0 + b * n + c, vcol)),
                  pl.BlockSpec((ch, vw), lambda b, c: (blk0 + b * n + c, gcol)),
                  pl.BlockSpec((ch, kw), lambda b, c: (c, 0)),
                  pl.BlockSpec((ch, kw), lambda b, c: (c, 0)),
                  pl.BlockSpec((ch, kw), lambda b, c: (0, 0)),
                  pl.BlockSpec((ch, kw), lambda b, c: (0, 0)),
                  pl.BlockSpec((H_B, ch, ch), lambda b, c: (0, 0, 0)),
                  pl.BlockSpec((kw, vw), lambda b, c: (0, 0)),
                  pl.BlockSpec((kw, vw), lambda b, c: (0, 0)),
                  pl.BlockSpec((1, kw, vw), lambda b, c: (b, 0, 0))],
        out_specs=[pl.BlockSpec((ch, vw), lambda b, c: (b * n + c, 0)),
                   pl.BlockSpec((1, kw, vw), lambda b, c: (b, 0, 0))],
        out_shape=[jax.ShapeDtypeStruct((batch * seq, vw), F32),
                   jax.ShapeDtypeStruct((batch, kw, vw), F32)],
        scratch_shapes=[pltpu.VMEM((kw, vw), F32)],
        compiler_params=_cparams(("parallel", "arbitrary")),
        name="retention",
    )(h0, h0, h0, h0, cos, sin, qdec, kdec, dmat, gc, bd, s0_bd)
    return out, s_fin


def _state_to_bd(s):
    b = s.shape[0]
    eye = jnp.eye(H_B, dtype=s.dtype)
    return (s[:, :, :, None, :] * eye[None, :, None, :, None]).reshape(b, H_B * DKB, H_B * DVB)


def _bd_to_state(sbd):
    b = sbd.shape[0]
    s5 = sbd.reshape(b, H_B, DKB, H_B, DVB)
    return jnp.stack([s5[:, h, :, h, :] for h in range(H_B)], axis=1)


def _rwkv_prep_kernel(pc_ref, prev_ref, mu_ref, w0_ref, a0_ref, kkw_ref, kaw_ref, rk_ref,
                      w2_ref, a2_ref, g2_ref, ones_ref,
                      r_ref, w_ref, k_ref, v_ref, kk_ref, b_ref, g_ref, bonus_ref):
    pc = pc_ref[...]
    pm = pc + (prev_ref[...] - pc) * mu_ref[...]
    r = pm[:, 0:C_W]
    k = pm[:, C_W:2 * C_W]
    v = pm[:, 2 * C_W:3 * C_W]
    wa = pm[:, 3 * C_W:3 * C_W + W_LORA + A_LORA]
    gd = pm[:, 3 * C_W + W_LORA + A_LORA:]
    wl = jnp.dot(jnp.tanh(wa).astype(BF16), w2_ref[...], preferred_element_type=F32)
    w_log = -_softplus(-(w0_ref[...] + wl)) - 0.5
    decay = jnp.exp(-jnp.exp(w_log))
    a = _sigmoid(a0_ref[...] + jnp.dot(wa.astype(BF16), a2_ref[...], preferred_element_type=F32))
    g = jnp.dot(_sigmoid(gd).astype(BF16), g2_ref[...], preferred_element_type=F32)

    def seg_sum(x):
        return jnp.dot(x, ones_ref[...], precision=HIGHEST, preferred_element_type=F32)

    kkr = k * kkw_ref[...]
    kk = kkr / jnp.maximum(jnp.sqrt(seg_sum(kkr * kkr)), 1e-12)
    k2 = k * (1.0 + (a - 1.0) * kaw_ref[...])
    r_ref[...] = r
    w_ref[...] = decay
    k_ref[...] = k2
    v_ref[...] = v
    kk_ref[...] = kk
    b_ref[...] = kk * a
    g_ref[...] = g
    bonus_ref[...] = seg_sum(r * k2 * rk_ref[...]) * v


def _rwkv_prep(h1, prev, mu_p, w0, a0, kkw, kaw, rk, w2p, a2p, g2, ones_bd):
    m = h1.shape[0]
    tm = _pick_tile(m, 320)
    row = lambda x: x.reshape(1, -1)
    tok = lambda w: pl.BlockSpec((tm, w), lambda i: (i, 0))
    full = lambda a: pl.BlockSpec(a.shape, lambda i: (0, 0))
    small = [row(mu_p), row(w0), row(a0), row(kkw), row(kaw), row(rk), w2p, a2p, g2, ones_bd]
    outs = pl.pallas_call(
        _rwkv_prep_kernel,
        grid=(m // tm,),
        in_specs=[tok(C_PROJ), tok(C_PROJ)] + [full(a) for a in small],
        out_specs=[tok(C_W)] * 8,
        out_shape=[jax.ShapeDtypeStruct((m, C_W), F32)] * 8,
        compiler_params=_cparams(("parallel",)),
        name="rwkv_prep",
    )(h1, prev, *small)
    return outs


def _rwkv_scan_kernel(r_ref, w_ref, k_ref, kk_ref, b_ref, vt_ref, s0_ref, ot_ref, sout_ref, s_ref, *, tc):
    tb = pl.program_id(1)

    @pl.when(tb == 0)
    def _():
        s_ref[...] = s0_ref[...]

    nb = s_ref.shape[0]
    npair = s_ref.shape[1]
    lane = lax.broadcasted_iota(jnp.int32, (NC, 2 * NC), 1)
    lo = lane < NC
    tlane = lax.broadcasted_iota(jnp.int32, (NC, tc), 1)
    for bi in range(nb):
        for p in range(npair):
            sl = slice(p * 2 * NC, (p + 1) * 2 * NC)
            state = s_ref[bi, p]
            ot_e = jnp.zeros((NC, tc), F32)
            ot_o = jnp.zeros((NC, tc), F32)
            for t in range(tc):
                kk_row = kk_ref[bi, t:t + 1, sl]
                prod = state * kk_row
                sa_e = jnp.sum(jnp.where(lo, prod, 0.0), axis=1, keepdims=True)
                sa_o = jnp.sum(jnp.where(lo, 0.0, prod), axis=1, keepdims=True)
                sa = jnp.where(lo, sa_e, sa_o)
                vcol = jnp.where(lo, vt_ref[0, bi, 2 * p, :, t:t + 1], vt_ref[0, bi, 2 * p + 1, :, t:t + 1])
                state = state * w_ref[bi, t:t + 1, sl] - sa * b_ref[bi, t:t + 1, sl] + vcol * k_ref[bi, t:t + 1, sl]
                outp = state * r_ref[bi, t:t + 1, sl]
                o_e = jnp.sum(jnp.where(lo, outp, 0.0), axis=1, keepdims=True)
                o_o = jnp.sum(jnp.where(lo, 0.0, outp), axis=1, keepdims=True)
                ot_e = jnp.where(tlane == t, o_e, ot_e)
                ot_o = jnp.where(tlane == t, o_o, ot_o)
            s_ref[bi, p] = state
            ot_ref[0, bi, 2 * p] = ot_e
            ot_ref[0, bi, 2 * p + 1] = ot_o

    @pl.when(tb == pl.num_programs(1) - 1)
    def _():
        sout_ref[...] = s_ref[...]


def _rwkv_scan(r, w, k, kk, b, v, s0, batch, seq):
    tc = SCAN_TC
    assert seq % tc == 0 and batch % 2 == 0
    nt = seq // tc
    npair = H_C // 2
    rows3 = lambda x: x.reshape(batch, seq, C_W)
    vt = v.reshape(batch, nt, tc, H_C, NC).transpose(1, 0, 3, 4, 2)
    s0p = s0.reshape(batch, npair, 2, NC, NC).transpose(0, 1, 3, 2, 4).reshape(batch, npair, NC, 2 * NC)
    row_spec = pl.BlockSpec((2, tc, C_W), lambda bb, tb: (bb, tb, 0))
    st_spec = pl.BlockSpec((2, npair, NC, 2 * NC), lambda bb, tb: (bb, 0, 0, 0))
    vt_spec = pl.BlockSpec((1, 2, H_C, NC, tc), lambda bb, tb: (tb, bb, 0, 0, 0))
    ot, s_fin = pl.pallas_call(
        functools.partial(_rwkv_scan_kernel, tc=tc),
        grid=(batch // 2, nt),
        in_specs=[row_spec] * 5 + [vt_spec, st_spec],
        out_specs=[vt_spec, st_spec],
        out_shape=[jax.ShapeDtypeStruct((nt, batch, H_C, NC, tc), F32),
                   jax.ShapeDtypeStruct((batch, npair, NC, 2 * NC), F32)],
        scratch_shapes=[pltpu.VMEM((2, npair, NC, 2 * NC), F32)],
        compiler_params=_cparams(("parallel", "arbitrary")),
        name="rwkv_scan",
    )(rows3(r), rows3(w), rows3(k), rows3(kk), rows3(b), vt, s0p)
    o = ot.transpose(1, 0, 4, 2, 3).reshape(batch * seq, C_W)
    s_new = s_fin.reshape(batch, npair, NC, 2, NC).transpose(0, 1, 3, 2, 4).reshape(batch, H_C, NC, NC)
    return o, s_new


def _rwkv_post_kernel(o_ref, bonus_ref, g_ref, lng_ref, lnb_ref, ones_ref, out_ref):
    def seg_mean(x):
        return jnp.dot(x, ones_ref[...], precision=HIGHEST, preferred_element_type=F32) * (1.0 / NC)

    o = o_ref[...]
    oc = o - seg_mean(o)
    on = oc * lax.rsqrt(seg_mean(oc * oc) + GN_EPS_C)
    out_ref[...] = (on * lng_ref[...] + lnb_ref[...] + bonus_ref[...]) * g_ref[...]


def _rwkv_post(o, bonus, g, ln_g, ln_b, ones_bd):
    m = o.shape[0]
    tm = _pick_tile(m, 640)
    tok = pl.BlockSpec((tm, C_W), lambda i: (i, 0))
    vec = pl.BlockSpec((1, C_W), lambda i: (0, 0))
    return pl.pallas_call(
        _rwkv_post_kernel,
        grid=(m // tm,),
        in_specs=[tok, tok, tok, vec, vec, pl.BlockSpec((C_W, C_W), lambda i: (0, 0))],
        out_specs=tok,
        out_shape=jax.ShapeDtypeStruct((m, C_W), F32),
        compiler_params=_cparams(("parallel",)),
        name="rwkv_post",
    )(o, bonus, g, ln_g.reshape(1, C_W), ln_b.reshape(1, C_W), ones_bd)


def _perm_odd_columns():
    offs = np.cumsum([0, C_W, W_LORA, C_W, C_W, A_LORA, G_LORA])
    seg = lambda i: np.arange(offs[i], offs[i + 1])
    pc_perm = np.concatenate([seg(0), seg(2), seg(3), seg(1), seg(4), seg(5)])
    return pc_perm


def kernel(x_prompt, x_sample, cache_a_k, cache_a_v, state_b, state_c_wkv, state_c_shift, cache_d_k, cache_d_v,
           cache_d_logf, page_table, rel_bias, w_in_even, lam_params, subln_g, w_out_even, w_in_odd, c_mu, c_w0,
           c_w2, c_a0, c_a2, c_g2, c_kk, c_ka, c_rk, c_ln_g, c_ln_b, d_bf, w_out_odd, w_router, b_router, w_gate,
           w_up, w_down, ln1_g, ln1_b, ln2_g, ln2_b):
    batch, seq, d = x_prompt.shape
    dbatch, tdec, _ = x_sample.shape
    n_pages = page_table.shape[1]
    n_pool = cache_a_k.shape[1]
    past = n_pages * PAGE
    bt = batch * seq
    ds = dbatch * tdec
    pt_flat = page_table.reshape(-1).astype(jnp.int32)
    gp = _pick_tile(n_pages, PAGES_PER_STEP, 1)

    x = jnp.concatenate([x_prompt.reshape(bt, d), x_sample.reshape(ds, d)], axis=0)
    outs = {}

    def moe_block(x_in, layer):
        gates = _router(x_in, w_router, b_router)
        wgu = jnp.concatenate([w_gate[layer], w_up[layer]], axis=-1).astype(BF16)
        return _moe_ln(x_in, gates, wgu, w_down[layer].astype(BF16), ln2_g[layer], ln2_b[layer])

    lam_init0 = 0.8 - 0.6 * math.exp(-0.3 * 0)
    h0 = _proj(x, w_in_even[0].astype(BF16))
    ka_off = H_A * DKA
    va_off = 2 * H_A * DKA
    ka_all = h0[:, ka_off:ka_off + H_A * DKA]
    va_all = h0[:, va_off:va_off + H_A * DVA]
    outs["a_k_p"] = ka_all[:bt].reshape(1, batch, seq, H_A, DKA)
    outs["a_v_p"] = va_all[:bt].reshape(1, batch, seq, H_A, DVA)
    outs["a_k_s"] = ka_all[bt:].reshape(1, dbatch, tdec, H_A, DKA)
    outs["a_v_s"] = va_all[bt:].reshape(1, dbatch, tdec, H_A, DVA)

    oa_p = _diff_prompt(h0, batch, seq, rel_bias, lam_params[0], subln_g[0], lam_init0)
    qa_s = h0[bt:, 0:H_A * DKA].reshape(dbatch, tdec, H_A, DKA) * (DA ** -0.5)
    wq_a = _block_diag_queries(qa_s, 2).astype(BF16)
    tab_d, tself_d = _diff_decode_tables(rel_bias, n_pages, gp, tdec)
    oa_s = _paged_attention(
        "diff", wq_a, cache_a_k[0].reshape(n_pool, PAGE, H_A * DKA), cache_a_v[0].reshape(n_pool, PAGE, H_A * DVA),
        pt_flat, n_pages, ka_all[bt:].reshape(dbatch, tdec, H_A * DKA), va_all[bt:].reshape(dbatch, tdec, H_A * DVA),
        (tab_d, tself_d, lam_params[0], subln_g[0].reshape(1, DVA)), 1.0, lam_init0)
    ob_p, sb_p = _retention(h0, 0, batch, seq, 0, jnp.zeros((batch, H_B * DKB, H_B * DVB), F32))
    ob_s, sb_s = _retention(h0, bt, dbatch, tdec, past, _state_to_bd(state_b[0].astype(F32)))
    outs["b_p"] = _bd_to_state(sb_p)[None]
    outs["b_s"] = _bd_to_state(sb_s)[None]
    mix_l = jnp.concatenate([oa_p, oa_s.reshape(ds, H_A * DVA)], axis=0)
    mix_r = jnp.concatenate([ob_p, ob_s], axis=0)
    x = _outproj_ln(mix_l, mix_r, x, w_out_even[0].astype(BF16), ln1_g[0], ln1_b[0])
    x = moe_block(x, 0)

    pc_perm = _perm_odd_columns()
    w_odd = w_in_odd[0]
    w_odd_p = jnp.concatenate([w_odd[:, pc_perm], w_odd[:, C_PROJ:],
                               jnp.zeros((d, ODD_IN_PAD - w_odd.shape[1]), w_odd.dtype)], axis=1).astype(BF16)
    h1 = _proj(x, w_odd_p)
    q_off = C_PROJ
    k_off = C_PROJ + D_W
    v_off = C_PROJ + 2 * D_W
    kd_all = h1[:, k_off:k_off + D_W]
    vd_all = h1[:, v_off:v_off + D_W]
    outs["d_k_p"] = kd_all[:bt].reshape(1, batch, seq, H_D, DD)
    outs["d_v_p"] = vd_all[:bt].reshape(1, batch, seq, H_D, DD)
    outs["d_k_s"] = kd_all[bt:].reshape(1, dbatch, tdec, H_D, DD)
    outs["d_v_s"] = vd_all[bt:].reshape(1, dbatch, tdec, H_D, DD)

    bf_pad = jnp.zeros((1, LANES), F32).at[0, :H_D].set(d_bf[0].astype(F32))
    lf_p, c_p = _fox_prep(h1, bf_pad, 0, batch, seq)
    lf_s, c_s = _fox_prep(h1, bf_pad, bt, dbatch, tdec)
    outs["d_lf_p"] = lf_p[:, :H_D].reshape(1, batch, seq, H_D)
    outs["d_lf_s"] = lf_s[:, :H_D].reshape(1, dbatch, tdec, H_D)
    od_p = _fox_prompt(h1, c_p, batch, seq)
    cn = c_s[:, :H_D].reshape(dbatch, tdec, H_D)
    cnt = cn.transpose(0, 2, 1)
    rho = cnt.reshape(dbatch, H_D * tdec, 1)
    logf_pool_t = cache_d_logf[0].astype(F32).transpose(0, 2, 1)
    kap = _suffix_sums(logf_pool_t, pt_flat, dbatch, n_pages)
    qd_s = h1[bt:, q_off:q_off + D_W].reshape(dbatch, tdec, H_D, DD)
    wq_d = _block_diag_queries(qd_s, 1).astype(BF16)
    od_s = _paged_attention(
        "fox", wq_d, cache_d_k[0].reshape(n_pool, PAGE, D_W), cache_d_v[0].reshape(n_pool, PAGE, D_W),
        pt_flat, n_pages, kd_all[bt:].reshape(dbatch, tdec, D_W), vd_all[bt:].reshape(dbatch, tdec, D_W),
        (kap, rho, cnt), DD ** -0.5)

    pc_p = h1[:bt, :C_PROJ].reshape(batch, seq, C_PROJ)
    pc_s = h1[bt:, :C_PROJ].reshape(dbatch, tdec, C_PROJ)
    shift_p = jnp.zeros((batch, 1, C_PROJ), F32)
    shift_s = state_c_shift[0][:, pc_perm][:, None, :].astype(F32)
    prev = jnp.concatenate([
        jnp.concatenate([shift_p, pc_p[:, :-1]], axis=1).reshape(bt, C_PROJ),
        jnp.concatenate([shift_s, pc_s[:, :-1]], axis=1).reshape(ds, C_PROJ)], axis=0)
    inv_perm = np.argsort(pc_perm)
    outs["c_shift_p"] = pc_p[:, -1][:, inv_perm][None]
    outs["c_shift_s"] = pc_s[:, -1][:, inv_perm][None]
    w2p = jnp.concatenate([c_w2[0], jnp.zeros((A_LORA, C_W), F32)], axis=0).astype(BF16)
    a2p = jnp.concatenate([jnp.zeros((W_LORA, C_W), F32), c_a2[0]], axis=0).astype(BF16)
    ones_bd = jnp.asarray(np.kron(np.eye(H_C, dtype=np.float32), np.ones((NC, NC), np.float32)))
    r_, w_, k_, v_, kk_, b_, g_, bonus = _rwkv_prep(
        h1, prev, c_mu[0][pc_perm], c_w0[0], c_a0[0], c_kk[0], c_ka[0], c_rk[0].reshape(-1),
        w2p, a2p, c_g2[0].astype(BF16), ones_bd)
    o_p, sc_p = _rwkv_scan(r_[:bt], w_[:bt], k_[:bt], kk_[:bt], b_[:bt], v_[:bt],
                           jnp.zeros((batch, H_C, NC, NC), F32), batch, seq)
    o_s, sc_s = _rwkv_scan(r_[bt:], w_[bt:], k_[bt:], kk_[bt:], b_[bt:], v_[bt:],
                           state_c_wkv[0].astype(F32), dbatch, tdec)
    outs["c_wkv_p"] = sc_p[None]
    outs["c_wkv_s"] = sc_s[None]
    oc = _rwkv_post(jnp.concatenate([o_p, o_s], axis=0), bonus, g_, c_ln_g[0], c_ln_b[0], ones_bd)
    mix_r = jnp.concatenate([od_p, od_s.reshape(ds, D_W)], axis=0)
    x = _outproj_ln(oc, mix_r, x, w_out_odd[0].astype(BF16), ln1_g[1], ln1_b[1])
    x = moe_block(x, 1)

    y_prompt = x[:bt].reshape(batch, seq, d)
    y_sample = x[bt:].reshape(dbatch, tdec, d)
    return (y_prompt, y_sample,
            outs["a_k_p"], outs["a_v_p"], outs["a_k_s"], outs["a_v_s"],
            outs["b_p"], outs["b_s"],
            outs["c_wkv_p"], outs["c_wkv_s"], outs["c_shift_p"], outs["c_shift_s"],
            outs["d_k_p"], outs["d_v_p"], outs["d_lf_p"], outs["d_k_s"], outs["d_v_s"], outs["d_lf_s"])
```

```python
import functools
import math

import numpy as np
import jax
import jax.numpy as jnp
from jax import lax
from jax.experimental import pallas as pl
from jax.experimental.pallas import tpu as pltpu

F32 = jnp.float32
BF16 = jnp.bfloat16
HIGHEST = lax.Precision.HIGHEST

D_MODEL = 1024
DEPTH = 2
PAGE = 128
MIX_W = D_MODEL // 2
DA = 64
DKA = 2 * DA
DVA = 2 * DA
H_A = MIX_W // DVA
DKB = 64
DVB = 128
H_B = MIX_W // DVB
RET_CHUNK = 128
NC = 64
H_C = MIX_W // NC
W_LORA = 64
A_LORA = 64
G_LORA = 128
GN_EPS_C = 64e-5
DD = 128
H_D = MIX_W // DD
N_BUCKETS = 32
MAX_DISTANCE = 128
N_EXPERTS = 16
N_GROUPS = 4
EPG = N_EXPERTS // N_GROUPS
D_EXPERT = D_MODEL // 4
DN_ALPHA = (2 * DEPTH) ** 0.25
LN_EPS = 1e-5
RMS_EPS = 1e-5
C_W = H_C * NC
C_PROJ = C_W + W_LORA + C_W + C_W + A_LORA + G_LORA
D_W = H_D * DD
EVEN_IN = 2 * H_A * DKA + H_A * DVA + 2 * H_B * DKB + 2 * H_B * DVB
ODD_IN_PAD = C_PROJ + 3 * D_W + 128

LANES = 128
VMEM_LIMIT = 56 * 1024 * 1024
NEG = -1e30
ATT_TQ = 256
ATT_FAR_TILES = 4
PAGES_PER_STEP = 8
SCAN_TC = 8


def _cparams(sem):
    return pltpu.CompilerParams(dimension_semantics=sem, vmem_limit_bytes=VMEM_LIMIT)


def _pick_tile(m, cap, mult=8):
    best = None
    for t in range(mult, min(m, cap) + 1, mult):
        if m % t == 0:
            best = t
    assert best is not None, (m, cap, mult)
    return best


def _layer_norm_rows(z, g, b):
    mu = jnp.mean(z, axis=-1, keepdims=True)
    zc = z - mu
    var = jnp.mean(zc * zc, axis=-1, keepdims=True)
    return zc * lax.rsqrt(var + LN_EPS) * g + b


def _sigmoid(x):
    return 1.0 / (1.0 + jnp.exp(-x))


def _softplus(y):
    return jnp.maximum(y, 0.0) + jnp.log1p(jnp.exp(-jnp.abs(y)))


def _proj_kernel(x_ref, w_ref, o_ref):
    o_ref[...] = jnp.dot(x_ref[...].astype(BF16), w_ref[...], preferred_element_type=F32)


def _proj(x, w_bf):
    m, k = x.shape
    n = w_bf.shape[1]
    tm = _pick_tile(m, 320)
    return pl.pallas_call(
        _proj_kernel,
        grid=(m // tm,),
        in_specs=[pl.BlockSpec((tm, k), lambda i: (i, 0)),
                  pl.BlockSpec((k, n), lambda i: (0, 0))],
        out_specs=pl.BlockSpec((tm, n), lambda i: (i, 0)),
        out_shape=jax.ShapeDtypeStruct((m, n), F32),
        compiler_params=_cparams(("parallel",)),
        name="proj",
    )(x, w_bf)


def _outproj_kernel(a_ref, b_ref, x_ref, w_ref, g_ref, be_ref, o_ref):
    half = a_ref.shape[1]
    y = jnp.dot(a_ref[...].astype(BF16), w_ref[0:half, :], preferred_element_type=F32)
    y = y + jnp.dot(b_ref[...].astype(BF16), w_ref[half:2 * half, :], preferred_element_type=F32)
    z = DN_ALPHA * x_ref[...] + y
    o_ref[...] = _layer_norm_rows(z, g_ref[...], be_ref[...])


def _outproj_ln(mix_l, mix_r, x, w_bf, g, b):
    m, d = x.shape
    half = mix_l.shape[1]
    tm = _pick_tile(m, 640)
    return pl.pallas_call(
        _outproj_kernel,
        grid=(m // tm,),
        in_specs=[pl.BlockSpec((tm, half), lambda i: (i, 0)),
                  pl.BlockSpec((tm, half), lambda i: (i, 0)),
                  pl.BlockSpec((tm, d), lambda i: (i, 0)),
                  pl.BlockSpec((2 * half, d), lambda i: (0, 0)),
                  pl.BlockSpec((1, d), lambda i: (0, 0)),
                  pl.BlockSpec((1, d), lambda i: (0, 0))],
        out_specs=pl.BlockSpec((tm, d), lambda i: (i, 0)),
        out_shape=jax.ShapeDtypeStruct((m, d), F32),
        compiler_params=_cparams(("parallel",)),
        name="outproj_ln",
    )(mix_l, mix_r, x, w_bf, g.reshape(1, d), b.reshape(1, d))


def _router_kernel(x_ref, wr_ref, br_ref, g_ref):
    logits = _nt_dot(wr_ref[...], x_ref[...].astype(BF16))
    mx = jnp.max(logits, axis=0, keepdims=True)
    ex = jnp.exp(logits - mx)
    scores = ex / jnp.sum(ex, axis=0, keepdims=True)
    sel = scores + br_ref[...]
    rows = [sel[e:e + 1, :] for e in range(N_EXPERTS)]
    srow = [scores[e:e + 1, :] for e in range(N_EXPERTS)]
    grp = []
    for gi in range(N_GROUPS):
        a, b, c, d = rows[EPG * gi:EPG * gi + EPG]
        hi1, lo1 = jnp.maximum(a, b), jnp.minimum(a, b)
        hi2, lo2 = jnp.maximum(c, d), jnp.minimum(c, d)
        top1 = jnp.maximum(hi1, hi2)
        top2 = jnp.maximum(jnp.minimum(hi1, hi2), jnp.maximum(lo1, lo2))
        grp.append(top1 + top2)
    best_v = grp[0]
    best_i = jnp.zeros_like(best_v, dtype=jnp.int32)
    for gi in range(1, N_GROUPS):
        better = grp[gi] > best_v
        best_v = jnp.where(better, grp[gi], best_v)
        best_i = jnp.where(better, gi, best_i)
    chosen = []
    for e in range(N_EXPERTS):
        gi = e // EPG
        rank = jnp.zeros_like(best_i)
        for j in range(EPG * gi, EPG * gi + EPG):
            if j == e:
                continue
            beats = (rows[j] >= rows[e]) if j < e else (rows[j] > rows[e])
            rank = rank + beats.astype(jnp.int32)
        chosen.append((best_i == gi) & (rank < 2))
    wsum = jnp.zeros_like(best_v)
    for e in range(N_EXPERTS):
        wsum = wsum + jnp.where(chosen[e], srow[e], 0.0)
    for e in range(N_EXPERTS):
        g_ref[e:e + 1, :] = jnp.where(chosen[e], srow[e] / wsum, 0.0)


def _router(x, w_router, b_router):
    m, d = x.shape
    tm = _pick_tile(m, 1280, LANES)
    gt = pl.pallas_call(
        _router_kernel,
        grid=(m // tm,),
        in_specs=[pl.BlockSpec((tm, d), lambda i: (i, 0)),
                  pl.BlockSpec((N_EXPERTS, d), lambda i: (0, 0)),
                  pl.BlockSpec((N_EXPERTS, 1), lambda i: (0, 0))],
        out_specs=pl.BlockSpec((N_EXPERTS, tm), lambda i: (0, i)),
        out_shape=jax.ShapeDtypeStruct((N_EXPERTS, m), F32),
        compiler_params=_cparams(("parallel",)),
        name="router",
    )(x, w_router.T.astype(BF16), b_router.reshape(N_EXPERTS, 1))
    return gt.T


def _moe_kernel(x_ref, gate_ref, wgu_ref, wd_ref, g_ref, b_ref, o_ref, xb_ref, acc_ref):
    e = pl.program_id(1)

    @pl.when(e == 0)
    def _():
        xb_ref[...] = x_ref[...].astype(BF16)
        acc_ref[...] = jnp.zeros_like(acc_ref)

    h = jnp.dot(xb_ref[...], wgu_ref[0], preferred_element_type=F32)
    gates = gate_ref[...]
    lane = lax.broadcasted_iota(jnp.int32, gates.shape, 1)
    gcol = jnp.sum(jnp.where(lane == e, gates, 0.0), axis=1, keepdims=True)
    hg = h[:, :D_EXPERT]
    hu = h[:, D_EXPERT:]
    act = hg * _sigmoid(hg) * hu * gcol
    acc_ref[...] += jnp.dot(act.astype(BF16), wd_ref[0], preferred_element_type=F32)

    @pl.when(e == N_EXPERTS - 1)
    def _():
        z = DN_ALPHA * x_ref[...] + acc_ref[...]
        o_ref[...] = _layer_norm_rows(z, g_ref[...], b_ref[...])


def _moe_ln(x, gates, wgu_bf, wd_bf, g, b):
    m, d = x.shape
    tm = _pick_tile(m, 640)
    return pl.pallas_call(
        _moe_kernel,
        grid=(m // tm, N_EXPERTS),
        in_specs=[pl.BlockSpec((tm, d), lambda i, e: (i, 0)),
                  pl.BlockSpec((tm, N_EXPERTS), lambda i, e: (i, 0)),
                  pl.BlockSpec((1, d, 2 * D_EXPERT), lambda i, e: (e, 0, 0)),
                  pl.BlockSpec((1, D_EXPERT, d), lambda i, e: (e, 0, 0)),
                  pl.BlockSpec((1, d), lambda i, e: (0, 0)),
                  pl.BlockSpec((1, d), lambda i, e: (0, 0))],
        out_specs=pl.BlockSpec((tm, d), lambda i, e: (i, 0)),
        out_shape=jax.ShapeDtypeStruct((m, d), F32),
        scratch_shapes=[pltpu.VMEM((tm, d), BF16), pltpu.VMEM((tm, d), F32)],
        compiler_params=_cparams(("parallel", "arbitrary")),
        name="moe_ln",
    )(x, gates, wgu_bf, wd_bf, g.reshape(1, d), b.reshape(1, d))


def _softmax_update(s, v_bf, m_ref, l_ref, acc_ref, idx):
    m_prev = m_ref[idx][:, 0:1]
    l_prev = l_ref[idx][:, 0:1]
    m_new = jnp.maximum(m_prev, jnp.max(s, axis=1, keepdims=True))
    alpha = jnp.exp(m_prev - m_new)
    p = jnp.exp(s - m_new)
    l_new = alpha * l_prev + jnp.sum(p, axis=1, keepdims=True)
    acc_ref[idx] = alpha * acc_ref[idx] + jnp.dot(p.astype(BF16), v_bf, preferred_element_type=F32)
    m_ref[idx] = jnp.broadcast_to(m_new, m_ref.shape[1:])
    l_ref[idx] = jnp.broadcast_to(l_new, l_ref.shape[1:])


def _softmax_init(m_ref, l_ref, acc_ref):
    m_ref[...] = jnp.full(m_ref.shape, NEG, F32)
    l_ref[...] = jnp.zeros(l_ref.shape, F32)
    acc_ref[...] = jnp.zeros(acc_ref.shape, F32)


def _far_loops(nfar, body):
    nwide = nfar // ATT_FAR_TILES

    def wide(j, carry):
        body(j, ATT_FAR_TILES)
        return carry

    def single(j, carry):
        body(j, 1)
        return carry

    lax.fori_loop(0, nwide, wide, 0)
    lax.fori_loop(nwide * ATT_FAR_TILES, nfar, single, 0)


def _nt_dot(a, b):
    return lax.dot_general(a, b, (((1,), (1,)), ((), ())), preferred_element_type=F32)


def _tn_dot(a, b):
    return lax.dot_general(a, b, (((0,), (0,)), ((), ())), preferred_element_type=F32)


def _t5_bucket_np(dist):
    max_exact = N_BUCKETS // 2
    d = np.maximum(dist, 0)
    ratio = (np.maximum(d, 1).astype(np.float32) / np.float32(max_exact)).astype(np.float32)
    log_ratio = (np.log(ratio).astype(np.float32) / np.float32(math.log(MAX_DISTANCE / max_exact))).astype(np.float32)
    large = np.minimum(max_exact + (log_ratio * np.float32(N_BUCKETS - max_exact)).astype(np.int32), N_BUCKETS - 1)
    return np.where(d < max_exact, d, large).astype(np.int32)


_T5_FAR = int(np.min(np.nonzero(_t5_bucket_np(np.arange(4 * MAX_DISTANCE)) == N_BUCKETS - 1)[0]))
assert np.all(_t5_bucket_np(np.arange(_T5_FAR, 8 * MAX_DISTANCE)) == N_BUCKETS - 1)


def _lam_from_params(lp):
    s01 = jnp.sum(lp[0:1, :] * lp[1:2, :], axis=1, keepdims=True)
    s23 = jnp.sum(lp[2:3, :] * lp[3:4, :], axis=1, keepdims=True)
    return jnp.exp(s01) - jnp.exp(s23)


def _diff_finalize(o0, o1, lam, gain, lam_init):
    o = o0 - lam * o1
    ms = jnp.mean(o * o, axis=-1, keepdims=True)
    return o * lax.rsqrt(ms + RMS_EPS) * gain * (1.0 - lam_init)


def _diff_prompt_kernel(q_ref, k_ref, v_ref, tab_ref, lp_ref, gain_ref, o_ref,
                        kb_ref, vb_ref, m_ref, l_ref, acc_ref, *, tq, lam_init):
    i = pl.program_id(2)

    @pl.when(i == 0)
    def _():
        kb_ref[...] = k_ref[...].astype(BF16)
        vb_ref[...] = v_ref[...].astype(BF16)

    _softmax_init(m_ref, l_ref, acc_ref)
    qf = q_ref[...] * (DA ** -0.5)
    lane = lax.broadcasted_iota(jnp.int32, qf.shape, 1)
    qm = (jnp.where(lane < DA, qf, 0.0).astype(BF16), jnp.where(lane >= DA, qf, 0.0).astype(BF16))

    def chunk(j, bias, tiles=1):
        width = tiles * tq
        ks = pl.multiple_of(j * width, width)
        kc = kb_ref[pl.ds(ks, width), :]
        vc = vb_ref[pl.ds(ks, width), :]
        for mi in range(2):
            s = _nt_dot(qm[mi], kc)
            if bias is not None:
                s = s + bias
            _softmax_update(s, vc, m_ref, l_ref, acc_ref, mi)

    _far_loops(jnp.maximum(i - 1, 0), lambda j, tiles: chunk(j, None, tiles))

    @pl.when(i >= 1)
    def _():
        chunk(i - 1, tab_ref[0, 1])

    chunk(i, tab_ref[0, 0])
    lam = _lam_from_params(lp_ref[...]) + lam_init
    o0 = acc_ref[0] / l_ref[0][:, 0:1]
    o1 = acc_ref[1] / l_ref[1][:, 0:1]
    o_ref[...] = _diff_finalize(o0, o1, lam, gain_ref[...], lam_init)


def _diff_tables(rel_bias, tq):
    qi = np.arange(tq)[:, None]
    kj = np.arange(tq)[None, :]
    rb = rel_bias.astype(F32)
    rel = rb - rb[N_BUCKETS - 1][None, :]
    diag = jnp.where(jnp.asarray(qi >= kj)[..., None], rel[_t5_bucket_np(qi - kj)], NEG)
    prev = rel[_t5_bucket_np(tq + qi - kj)]
    return jnp.moveaxis(jnp.stack([diag, prev], 0), -1, 0)


def _diff_prompt(h0, batch, seq, rel_bias, lam_params, subln_g, lam_init):
    tq = min(ATT_TQ, seq)
    assert seq % tq == 0 and tq + 1 >= _T5_FAR
    nq = seq // tq
    tab = _diff_tables(rel_bias, tq)
    kern = functools.partial(_diff_prompt_kernel, tq=tq, lam_init=lam_init)
    qoff, koff, voff = 0, H_A, 2 * H_A
    return pl.pallas_call(
        kern,
        grid=(batch, H_A, nq),
        in_specs=[pl.BlockSpec((tq, DKA), lambda b, h, i: (b * nq + i, qoff + h)),
                  pl.BlockSpec((seq, DKA), lambda b, h, i: (b, koff + h)),
                  pl.BlockSpec((seq, DVA), lambda b, h, i: (b, voff + h)),
                  pl.BlockSpec((1, 2, tq, tq), lambda b, h, i: (h, 0, 0, 0)),
                  pl.BlockSpec((4, DA), lambda b, h, i: (0, 0)),
                  pl.BlockSpec((1, DVA), lambda b, h, i: (0, 0))],
        out_specs=pl.BlockSpec((tq, DVA), lambda b, h, i: (b * nq + i, h)),
        out_shape=jax.ShapeDtypeStruct((batch * seq, H_A * DVA), F32),
        scratch_shapes=[pltpu.VMEM((seq, DKA), BF16), pltpu.VMEM((seq, DVA), BF16),
                        pltpu.VMEM((2, tq, LANES), F32), pltpu.VMEM((2, tq, LANES), F32),
                        pltpu.VMEM((2, tq, DVA), F32)],
        compiler_params=_cparams(("parallel", "parallel", "arbitrary")),
        name="diff_prompt",
    )(h0, h0, h0, tab, lam_params, subln_g.reshape(1, DVA))


def _fox_prompt_kernel(q_ref, k_ref, v_ref, c_ref, kap_ref, o_ref,
                       kb_ref, vb_ref, m_ref, l_ref, acc_ref, *, tq):
    h = pl.program_id(1)
    i = pl.program_id(2)

    @pl.when(i == 0)
    def _():
        kb_ref[...] = k_ref[...].astype(BF16)
        vb_ref[...] = v_ref[...].astype(BF16)

    _softmax_init(m_ref, l_ref, acc_ref)
    qb = q_ref[...].astype(BF16)
    cblk = c_ref[...]
    lane = lax.broadcasted_iota(jnp.int32, cblk.shape, 1)
    rho = jnp.sum(jnp.where(lane == h, cblk, 0.0), axis=1, keepdims=True)

    def chunk(j, masked, tiles=1):
        width = tiles * tq
        ks = pl.multiple_of(j * width, width)
        kc = kb_ref[pl.ds(ks, width), :]
        vc = vb_ref[pl.ds(ks, width), :]
        s = _nt_dot(qb, kc) * (DD ** -0.5) + (rho - kap_ref[0, :, pl.ds(ks, width)])
        if masked:
            r = lax.broadcasted_iota(jnp.int32, s.shape, 0)
            c = lax.broadcasted_iota(jnp.int32, s.shape, 1)
            s = jnp.where(r >= c, s, NEG)
        _softmax_update(s, vc, m_ref, l_ref, acc_ref, 0)

    _far_loops(i, lambda j, tiles: chunk(j, False, tiles))
    chunk(i, True)
    o_ref[...] = acc_ref[0] / l_ref[0][:, 0:1]


def _fox_prompt(h1, c_all, batch, seq):
    tq = min(ATT_TQ, seq)
    nq = seq // tq
    qoff = C_PROJ // LANES
    koff = qoff + H_D
    voff = koff + H_D
    kap = c_all[:batch * seq, :H_D].reshape(batch, seq, H_D).transpose(0, 2, 1).reshape(batch * H_D, 1, seq)
    kern = functools.partial(_fox_prompt_kernel, tq=tq)
    return pl.pallas_call(
        kern,
        grid=(batch, H_D, nq),
        in_specs=[pl.BlockSpec((tq, DD), lambda b, h, i: (b * nq + i, qoff + h)),
                  pl.BlockSpec((seq, DD), lambda b, h, i: (b, koff + h)),
                  pl.BlockSpec((seq, DD), lambda b, h, i: (b, voff + h)),
                  pl.BlockSpec((tq, LANES), lambda b, h, i: (b * nq + i, 0)),
                  pl.BlockSpec((1, 1, seq), lambda b, h, i: (b * H_D + h, 0, 0))],
        out_specs=pl.BlockSpec((tq, DD), lambda b, h, i: (b * nq + i, h)),
        out_shape=jax.ShapeDtypeStruct((batch * seq, D_W), F32),
        scratch_shapes=[pltpu.VMEM((seq, DD), BF16), pltpu.VMEM((seq, DD), BF16),
                        pltpu.VMEM((1, tq, LANES), F32), pltpu.VMEM((1, tq, LANES), F32),
                        pltpu.VMEM((1, tq, DD), F32)],
        compiler_params=_cparams(("parallel", "parallel", "arbitrary")),
        name="fox_prompt",
    )(h1, h1, h1, c_all, kap)


def _fox_prep_kernel(fd_ref, bf_ref, tri_ref, lf_ref, c_ref, carry_ref):
    @pl.when(pl.program_id(1) == 0)
    def _():
        carry_ref[...] = jnp.zeros_like(carry_ref)

    x = fd_ref[...] + bf_ref[...]
    lf = -_softplus(-x)
    inc = jnp.dot(tri_ref[...], lf, precision=HIGHEST, preferred_element_type=F32) + carry_ref[0:1, :]
    lf_ref[...] = lf
    c_ref[...] = inc
    tm = inc.shape[0]
    carry_ref[...] = jnp.broadcast_to(inc[tm - 1:tm, :], carry_ref.shape)


def _fox_prep(h1, bf_pad, row0, batch, seq):
    tm = _pick_tile(seq, 256)
    nt = seq // tm
    assert row0 % tm == 0
    blk0 = row0 // tm
    col = (ODD_IN_PAD - LANES) // LANES
    tri = jnp.asarray(np.tril(np.ones((tm, tm), np.float32)))
    return pl.pallas_call(
        _fox_prep_kernel,
        grid=(batch, nt),
        in_specs=[pl.BlockSpec((tm, LANES), lambda b, i: (blk0 + b * nt + i, col)),
                  pl.BlockSpec((1, LANES), lambda b, i: (0, 0)),
                  pl.BlockSpec((tm, tm), lambda b, i: (0, 0))],
        out_specs=[pl.BlockSpec((tm, LANES), lambda b, i: (b * nt + i, 0)),
                   pl.BlockSpec((tm, LANES), lambda b, i: (b * nt + i, 0))],
        out_shape=[jax.ShapeDtypeStruct((batch * seq, LANES), F32),
                   jax.ShapeDtypeStruct((batch * seq, LANES), F32)],
        scratch_shapes=[pltpu.VMEM((8, LANES), F32)],
        compiler_params=_cparams(("parallel", "arbitrary")),
        name="fox_prep",
    )(h1, bf_pad, tri)


def _suffix_kernel(pt_ref, *refs, gpages):
    lf_refs = refs[:gpages]
    ut_ref = refs[gpages]
    o_ref = refs[gpages + 1]
    carry_ref = refs[gpages + 2]

    @pl.when(pl.program_id(1) == 0)
    def _():
        carry_ref[...] = jnp.zeros_like(carry_ref)

    run = carry_ref[:, 0:1]
    for j in reversed(range(gpages)):
        lf = lf_refs[j][...]
        incl = jnp.dot(lf, ut_ref[...], precision=HIGHEST, preferred_element_type=F32)
        o_ref[0, :, j * PAGE:(j + 1) * PAGE] = incl - lf + run
        run = run + incl[:, 0:1]
    carry_ref[...] = jnp.broadcast_to(run, carry_ref.shape)


def _suffix_sums(logf_pool_t, page_table_flat, dbatch, n_pages):
    gp = _pick_tile(n_pages, PAGES_PER_STEP, 1)
    ng = n_pages // gp
    ut = jnp.asarray(np.tril(np.ones((PAGE, PAGE), np.float32)))

    def page_map(j):
        return lambda b, g, pt: (pt[b * n_pages + (ng - 1 - g) * gp + j], 0, 0)

    in_specs = [pl.BlockSpec((None, H_D, PAGE), page_map(j)) for j in range(gp)]
    in_specs.append(pl.BlockSpec((PAGE, PAGE), lambda b, g, pt: (0, 0)))
    return pl.pallas_call(
        functools.partial(_suffix_kernel, gpages=gp),
        grid_spec=pltpu.PrefetchScalarGridSpec(
            num_scalar_prefetch=1,
            grid=(dbatch, ng),
            in_specs=in_specs,
            out_specs=pl.BlockSpec((1, H_D, gp * PAGE), lambda b, g, pt: (b, 0, ng - 1 - g)),
            scratch_shapes=[pltpu.VMEM((H_D, LANES), F32)]),
        out_shape=jax.ShapeDtypeStruct((dbatch, H_D, n_pages * PAGE), F32),
        compiler_params=_cparams(("parallel", "arbitrary")),
        name="fox_suffix",
    )(page_table_flat, *([logf_pool_t] * gp), ut)


def _paged_kernel(pt_ref, *refs, gpages, tdec, mode, scale, lam_init, chunk, nheads):
    q_ref = refs[0]
    k_refs = refs[1:1 + gpages]
    v_refs = refs[1 + gpages:1 + 2 * gpages]
    rest = refs[1 + 2 * gpages:]
    if mode == "diff":
        tab_ref, tself_ref, kn_ref, vn_ref, lp_ref, gain_ref, o_ref, s_ref, vb_ref = rest
    else:
        kap_ref, rho_ref, cnt_ref, kn_ref, vn_ref, o_ref, s_ref, vb_ref = rest
    g = pl.program_id(1)
    ng = pl.num_programs(1)
    gw = gpages * PAGE
    rows, past = s_ref.shape
    rph = rows // nheads
    dh = q_ref.shape[2]
    off = pl.multiple_of(g * gw, gw)

    def head_rows(ref, h):
        return ref[pl.ds(h, PAGE, stride=nheads), :]

    for h in range(nheads):
        hs = slice(h * rph, (h + 1) * rph)
        kh = jnp.concatenate([head_rows(r, h).astype(BF16) for r in k_refs], axis=0)
        s = _nt_dot(q_ref[0, hs, :], kh)
        if scale != 1.0:
            s = s * scale
        if mode == "diff":
            s = s + tab_ref[0, hs, :]
        else:
            s = (s + rho_ref[0, hs, :]) + kap_ref[0, h:h + 1, :]
        s_ref[hs, pl.ds(off, gw)] = s
        for j in range(gpages):
            vb_ref[h, pl.ds(off + j * PAGE, PAGE), :] = head_rows(v_refs[j], h).astype(BF16)

    @pl.when(g == ng - 1)
    def _():
        tails = []
        for h in range(nheads):
            hs = slice(h * rph, (h + 1) * rph)
            kn = kn_ref[0, :, h * dh:(h + 1) * dh].astype(BF16)
            t = _nt_dot(q_ref[0, hs, :], kn)
            if scale != 1.0:
                t = t * scale
            if mode == "diff":
                t = t + tself_ref[hs, :]
            else:
                t = (t + rho_ref[0, hs, :]) - cnt_ref[0, h:h + 1, :]
                r = lax.broadcasted_iota(jnp.int32, t.shape, 0)
                c = lax.broadcasted_iota(jnp.int32, t.shape, 1)
                t = jnp.where(c <= r, t, NEG)
            tails.append(t)
        tail = jnp.concatenate(tails, axis=0)
        if mode == "diff":
            lam = _lam_from_params(lp_ref[...]) + lam_init
        nchunk = past // chunk

        def sc(ci, hs=slice(None)):
            return s_ref[hs, pl.ds(pl.multiple_of(ci * chunk, chunk), chunk)]

        m = lax.fori_loop(0, nchunk, lambda ci, mm: jnp.maximum(mm, jnp.max(sc(ci), axis=1, keepdims=True)),
                          jnp.max(tail, axis=1, keepdims=True))
        den = lax.fori_loop(0, nchunk, lambda ci, dd: dd + jnp.sum(jnp.exp(sc(ci) - m), axis=1, keepdims=True),
                            jnp.sum(jnp.exp(tail - m), axis=1, keepdims=True))

        def weights(sv, h):
            hs = slice(h * rph, (h + 1) * rph)
            p = jnp.exp(sv - m[hs]) / den[hs]
            if mode == "diff":
                p = p[0:tdec] - lam * p[tdec:2 * tdec]
            return p.astype(BF16)

        def pv_body(ci, accs):
            cs = pl.multiple_of(ci * chunk, chunk)
            return tuple(
                accs[h] + jnp.dot(weights(sc(ci, slice(h * rph, (h + 1) * rph)), h), vb_ref[h, pl.ds(cs, chunk), :],
                                  preferred_element_type=F32)
                for h in range(nheads))

        init = tuple(
            jnp.dot(weights(tails[h], h), vn_ref[0, :, h * dh:(h + 1) * dh].astype(BF16), preferred_element_type=F32)
            for h in range(nheads))
        outs = lax.fori_loop(0, nchunk, pv_body, init)
        for h in range(nheads):
            oh = outs[h]
            if mode == "diff":
                ms = jnp.mean(oh * oh, axis=-1, keepdims=True)
                oh = oh * lax.rsqrt(ms + RMS_EPS) * gain_ref[...] * (1.0 - lam_init)
            o_ref[0, :, h * dh:(h + 1) * dh] = oh


def _query_rows(q, n_maps):
    db, t, nh, w = q.shape
    dm = w // n_maps
    qh = q.transpose(0, 2, 1, 3)
    sel = jnp.asarray(np.kron(np.eye(n_maps, dtype=np.float32), np.ones((1, dm), np.float32)))
    rows = qh[:, :, None, :, :] * sel[None, None, :, None, :]
    return rows.reshape(db, nh * n_maps * t, w)


def _paged_attention(mode, q_rows_bf, k_pool, v_pool, page_table_flat, n_pages, k_new, v_new, extra, scale,
                     lam_init=0.0):
    dbatch, rows, dh = q_rows_bf.shape
    tdec = k_new.shape[1]
    width = k_new.shape[2]
    nheads = width // dh
    gp = _pick_tile(n_pages, PAGES_PER_STEP, 1)
    ng = n_pages // gp
    past = n_pages * PAGE
    chunk = _pick_tile(past, 2048, LANES)
    pad_rows = lambda x: jnp.pad(x, ((0, 0), (0, PAGE - tdec), (0, 0)))
    k_new = pad_rows(k_new)
    v_new = pad_rows(v_new)

    def page_map(j):
        return lambda b, g, pt: (pt[b * n_pages + g * gp + j], 0, 0)

    in_specs = [pl.BlockSpec((1, rows, dh), lambda b, g, pt: (b, 0, 0))]
    in_specs += [pl.BlockSpec((None, PAGE * nheads, dh), page_map(j)) for j in range(gp)]
    in_specs += [pl.BlockSpec((None, PAGE * nheads, dh), page_map(j)) for j in range(gp)]
    new_spec = pl.BlockSpec((1, PAGE, width), lambda b, g, pt: (b, 0, 0))
    if mode == "diff":
        tab, tself, lam_params, gain = extra
        tself = jnp.pad(tself, ((0, 0), (0, PAGE - tdec)), constant_values=NEG)
        in_specs += [pl.BlockSpec((1, rows, gp * PAGE), lambda b, g, pt: ((g + 1) // ng, 0, 0)),
                     pl.BlockSpec((rows, PAGE), lambda b, g, pt: (0, 0)),
                     new_spec, new_spec,
                     pl.BlockSpec((4, DA), lambda b, g, pt: (0, 0)),
                     pl.BlockSpec((1, DVA), lambda b, g, pt: (0, 0))]
        operands = [tab, tself, k_new, v_new, lam_params, gain]
    else:
        kap, rho, cnt = extra
        cnt = jnp.pad(cnt, ((0, 0), (0, 0), (0, PAGE - tdec)))
        in_specs += [pl.BlockSpec((1, H_D, gp * PAGE), lambda b, g, pt: (b, 0, g)),
                     pl.BlockSpec((1, rows, 1), lambda b, g, pt: (b, 0, 0)),
                     pl.BlockSpec((1, H_D, PAGE), lambda b, g, pt: (b, 0, 0)),
                     new_spec, new_spec]
        operands = [kap, rho, cnt, k_new, v_new]
    kern = functools.partial(_paged_kernel, gpages=gp, tdec=tdec, mode=mode, scale=scale, lam_init=lam_init,
                             chunk=chunk, nheads=nheads)
    return pl.pallas_call(
        kern,
        grid_spec=pltpu.PrefetchScalarGridSpec(
            num_scalar_prefetch=1,
            grid=(dbatch, ng),
            in_specs=in_specs,
            out_specs=pl.BlockSpec((1, tdec, width), lambda b, g, pt: (b, 0, 0)),
            scratch_shapes=[pltpu.VMEM((rows, past), F32), pltpu.VMEM((nheads, past, dh), BF16)]),
        out_shape=jax.ShapeDtypeStruct((dbatch, tdec, width), F32),
        compiler_params=_cparams(("parallel", "arbitrary")),
        name="paged_" + mode,
    )(page_table_flat, q_rows_bf, *([k_pool] * gp), *([v_pool] * gp), *operands)


def _diff_decode_tables(rel_bias, n_pages, gp, tdec):
    rb = rel_bias.astype(F32)
    rel = (rb - rb[N_BUCKETS - 1][None, :]).T
    past = n_pages * PAGE
    t = np.arange(tdec)[:, None]
    kpos = past - gp * PAGE + np.arange(gp * PAGE)[None, :]
    assert PAGE + 1 >= _T5_FAR
    last = rel[:, _t5_bucket_np(past + t - kpos)]
    s = np.arange(tdec)[None, :]
    selfb = jnp.where(jnp.asarray(t >= s)[None], rel[:, _t5_bucket_np(t - s)], NEG)
    rep = lambda x: jnp.broadcast_to(x[:, None], (H_A, 2) + x.shape[1:]).reshape((H_A * 2 * tdec,) + x.shape[2:])
    last_r = rep(last)
    tab = jnp.stack([jnp.zeros_like(last_r), last_r], 0)
    return tab, rep(selfb)


def _retention_kernel(q_ref, k_ref, v_ref, gb_ref, cos_ref, sin_ref, qdec_ref, kdec_ref, dmat_ref,
                      gc_ref, bd_ref, s0_ref, o_ref, sout_ref, s_ref):
    c = pl.program_id(1)

    @pl.when(c == 0)
    def _():
        s_ref[...] = s0_ref[0]

    cos = cos_ref[...]
    sin = sin_ref[...]
    width = cos.shape[1]
    lane = lax.broadcasted_iota(jnp.int32, cos.shape, 1)
    first_half = (lane % DKB) < (DKB // 2)

    def rot(x):
        swapped = jnp.where(first_half, -pltpu.roll(x, width - DKB // 2, 1), pltpu.roll(x, DKB // 2, 1))
        return x * cos + swapped * sin

    qr = rot(q_ref[...])
    kr = rot(k_ref[...]) * (DKB ** -0.5)
    v = v_ref[...]
    vb = v.astype(BF16)
    state = s_ref[...]
    cross = jnp.dot((qr * qdec_ref[...]).astype(BF16), state.astype(BF16), preferred_element_type=F32)
    lane_pair = lax.broadcasted_iota(jnp.int32, (qr.shape[0], LANES), 1)
    gate = gb_ref[...]
    for h in range(H_B):
        cb = (h * DKB) // LANES
        lo = (h * DKB) % LANES
        qh = qr[:, cb * LANES:(cb + 1) * LANES]
        qh = jnp.where((lane_pair >= lo) & (lane_pair < lo + DKB), qh, 0.0).astype(BF16)
        kh = kr[:, cb * LANES:(cb + 1) * LANES].astype(BF16)
        att = _nt_dot(qh, kh) * dmat_ref[h]
        oh = jnp.dot(att.astype(BF16), vb[:, h * DVB:(h + 1) * DVB], preferred_element_type=F32)
        oh = oh + cross[:, h * DVB:(h + 1) * DVB]
        ms = jnp.mean(oh * oh, axis=-1, keepdims=True)
        gh = gate[:, h * DVB:(h + 1) * DVB]
        o_ref[:, h * DVB:(h + 1) * DVB] = oh * lax.rsqrt(ms + RMS_EPS) * (gh * _sigmoid(gh))
    upd = _tn_dot((kr * kdec_ref[...]).astype(BF16), vb)
    new_state = state * gc_ref[...] + upd * bd_ref[...]
    s_ref[...] = new_state

    @pl.when(c == pl.num_programs(1) - 1)
    def _():
        sout_ref[0] = new_state


def _retention(h0, row0, batch, seq, start, s0_bd):
    ch = RET_CHUNK if seq % RET_CHUNK == 0 else seq
    n = seq // ch
    assert row0 % ch == 0
    blk0 = row0 // ch
    kw = H_B * DKB
    vw = H_B * DVB
    qcol = (2 * H_A * DKA + H_A * DVA) // kw
    kcol = qcol + 1
    vcol = (2 * H_A * DKA + H_A * DVA + 2 * kw) // vw
    gcol = vcol + 1
    half = DKB // 2
    inv = 1.0 / (10000.0 ** (jnp.arange(half, dtype=F32) / half))
    pos = (start + jnp.arange(seq)).astype(F32)
    ang = pos[:, None] * inv[None, :]
    cos = jnp.tile(jnp.cos(ang), (1, 2 * H_B))
    sin = jnp.tile(jnp.sin(ang), (1, 2 * H_B))
    log_g = jnp.log(1.0 - jnp.exp2(-5.0 - jnp.arange(H_B, dtype=F32)))
    idx = jnp.arange(ch, dtype=F32)
    diff = idx[:, None] - idx[None, :]
    dmat = jnp.where(diff >= 0, jnp.exp(log_g[:, None, None] * jnp.maximum(diff, 0.0)), 0.0)
    qdec = jnp.repeat(jnp.exp(log_g[None, :] * (idx[:, None] + 1.0)), DKB, axis=1)
    kdec = jnp.repeat(jnp.exp(log_g[None, :] * (ch - 1.0 - idx[:, None])), DKB, axis=1)
    gc = jnp.broadcast_to(jnp.repeat(jnp.exp(log_g * ch), DKB)[:, None], (kw, vw))
    bd = jnp.asarray(np.kron(np.eye(H_B, dtype=np.float32), np.ones((DKB, DVB), np.float32)))
    out, s_fin = pl.pallas_call(
        _retention_kernel,
        grid=(batch, n),
        in_specs=[pl.BlockSpec((ch, kw), lambda b, c: (blk0 + b * n + c, qcol)),
                  pl.BlockSpec((ch, kw), lambda b, c: (blk0 + b * n + c, kcol)),
                  pl.BlockSpec((ch, vw), lambda b, c: (blk0 + b * n + c, vcol)),
                  pl.BlockSpec((ch, vw), lambda b, c: (blk0 + b * n + c, gcol)),
                  pl.BlockSpec((ch, kw), lambda b, c: (c, 0)),
                  pl.BlockSpec((ch, kw), lambda b, c: (c, 0)),
                  pl.BlockSpec((ch, kw), lambda b, c: (0, 0)),
                  pl.BlockSpec((ch, kw), lambda b, c: (0, 0)),
                  pl.BlockSpec((H_B, ch, ch), lambda b, c: (0, 0, 0)),
                  pl.BlockSpec((kw, vw), lambda b, c: (0, 0)),
                  pl.BlockSpec((kw, vw), lambda b, c: (0, 0)),
                  pl.BlockSpec((1, kw, vw), lambda b, c: (b, 0, 0))],
        out_specs=[pl.BlockSpec((ch, vw), lambda b, c: (b * n + c, 0)),
                   pl.BlockSpec((1, kw, vw), lambda b, c: (b, 0, 0))],
        out_shape=[jax.ShapeDtypeStruct((batch * seq, vw), F32),
                   jax.ShapeDtypeStruct((batch, kw, vw), F32)],
        scratch_shapes=[pltpu.VMEM((kw, vw), F32)],
        compiler_params=_cparams(("parallel", "arbitrary")),
        name="retention",
    )(h0, h0, h0, h0, cos, sin, qdec, kdec, dmat, gc, bd, s0_bd)
    return out, s_fin


def _state_to_bd(s):
    b = s.shape[0]
    eye = jnp.eye(H_B, dtype=s.dtype)
    return (s[:, :, :, None, :] * eye[None, :, None, :, None]).reshape(b, H_B * DKB, H_B * DVB)


def _bd_to_state(sbd):
    b = sbd.shape[0]
    s5 = sbd.reshape(b, H_B, DKB, H_B, DVB)
    return jnp.stack([s5[:, h, :, h, :] for h in range(H_B)], axis=1)


def _rwkv_prep_kernel(pc_ref, prev_ref, mu_ref, w0_ref, a0_ref, kkw_ref, kaw_ref, rk_ref,
                      w2_ref, a2_ref, g2_ref, ones_ref,
                      r_ref, w_ref, k_ref, v_ref, kk_ref, b_ref, g_ref, bonus_ref):
    pc = pc_ref[...]
    pm = pc + (prev_ref[...] - pc) * mu_ref[...]
    r = pm[:, 0:C_W]
    k = pm[:, C_W:2 * C_W]
    v = pm[:, 2 * C_W:3 * C_W]
    wa = pm[:, 3 * C_W:3 * C_W + W_LORA + A_LORA]
    gd = pm[:, 3 * C_W + W_LORA + A_LORA:]
    wl = jnp.dot(jnp.tanh(wa).astype(BF16), w2_ref[...], preferred_element_type=F32)
    w_log = -_softplus(-(w0_ref[...] + wl)) - 0.5
    decay = jnp.exp(-jnp.exp(w_log))
    a = _sigmoid(a0_ref[...] + jnp.dot(wa.astype(BF16), a2_ref[...], preferred_element_type=F32))
    g = jnp.dot(_sigmoid(gd).astype(BF16), g2_ref[...], preferred_element_type=F32)

    def seg_sum(x):
        return jnp.dot(x, ones_ref[...], precision=HIGHEST, preferred_element_type=F32)

    kkr = k * kkw_ref[...]
    kk = kkr / jnp.maximum(jnp.sqrt(seg_sum(kkr * kkr)), 1e-12)
    k2 = k * (1.0 + (a - 1.0) * kaw_ref[...])
    r_ref[...] = r
    w_ref[...] = decay
    k_ref[...] = k2
    v_ref[...] = v
    kk_ref[...] = kk
    b_ref[...] = kk * a
    g_ref[...] = g
    bonus_ref[...] = seg_sum(r * k2 * rk_ref[...]) * v


def _rwkv_prep(h1, prev, mu_p, w0, a0, kkw, kaw, rk, w2p, a2p, g2, ones_bd):
    m = h1.shape[0]
    tm = _pick_tile(m, 320)
    row = lambda x: x.reshape(1, -1)
    tok = lambda w: pl.BlockSpec((tm, w), lambda i: (i, 0))
    full = lambda a: pl.BlockSpec(a.shape, lambda i: (0, 0))
    small = [row(mu_p), row(w0), row(a0), row(kkw), row(kaw), row(rk), w2p, a2p, g2, ones_bd]
    outs = pl.pallas_call(
        _rwkv_prep_kernel,
        grid=(m // tm,),
        in_specs=[tok(C_PROJ), tok(C_PROJ)] + [full(a) for a in small],
        out_specs=[tok(C_W)] * 8,
        out_shape=[jax.ShapeDtypeStruct((m, C_W), F32)] * 8,
        compiler_params=_cparams(("parallel",)),
        name="rwkv_prep",
    )(h1, prev, *small)
    return outs


def _rwkv_scan_kernel(r_ref, w_ref, k_ref, kk_ref, b_ref, vte_ref, vto_ref, s0_ref,
                      ote_ref, oto_ref, sout_ref, s_ref, *, tc):
    tb = pl.program_id(1)

    @pl.when(tb == 0)
    def _():
        s_ref[...] = s0_ref[...]

    nb, npair = s_ref.shape[0], s_ref.shape[1]
    g = nb * npair
    shape = (g, NC, 2 * NC)
    lo = lax.broadcasted_iota(jnp.int32, shape, 2) < NC
    tlane = lax.broadcasted_iota(jnp.int32, (g, NC, tc), 2)

    def rows(ref, t):
        return jnp.concatenate([ref[bi, t:t + 1, p * 2 * NC:(p + 1) * 2 * NC][None]
                                for bi in range(nb) for p in range(npair)], axis=0)

    def half_sums(x):
        even = jnp.sum(jnp.where(lo, x, 0.0), axis=2, keepdims=True)
        odd = jnp.sum(jnp.where(lo, 0.0, x), axis=2, keepdims=True)
        return even, odd

    state = s_ref[...].reshape(shape)
    vte = vte_ref[0].reshape(g, NC, tc)
    vto = vto_ref[0].reshape(g, NC, tc)
    ot_e = jnp.zeros((g, NC, tc), F32)
    ot_o = jnp.zeros((g, NC, tc), F32)
    for t in range(tc):
        sa_e, sa_o = half_sums(state * rows(kk_ref, t))
        sa = jnp.where(lo, sa_e, sa_o)
        vcol = jnp.where(lo, vte[:, :, t:t + 1], vto[:, :, t:t + 1])
        state = state * rows(w_ref, t) - sa * rows(b_ref, t) + vcol * rows(k_ref, t)
        o_e, o_o = half_sums(state * rows(r_ref, t))
        ot_e = jnp.where(tlane == t, o_e, ot_e)
        ot_o = jnp.where(tlane == t, o_o, ot_o)
    s_ref[...] = state.reshape(s_ref.shape)
    ote_ref[0] = ot_e.reshape(ote_ref.shape[1:])
    oto_ref[0] = ot_o.reshape(oto_ref.shape[1:])

    @pl.when(tb == pl.num_programs(1) - 1)
    def _():
        sout_ref[...] = s_ref[...]


def _rwkv_scan(r, w, k, kk, b, v, s0, batch, seq):
    tc = SCAN_TC
    assert seq % tc == 0 and batch % 2 == 0
    nt = seq // tc
    npair = H_C // 2
    rows3 = lambda x: x.reshape(batch, seq, C_W)
    vt = v.reshape(batch, nt, tc, npair, 2, NC).transpose(4, 1, 0, 3, 5, 2)
    s0p = s0.reshape(batch, npair, 2, NC, NC).transpose(0, 1, 3, 2, 4).reshape(batch, npair, NC, 2 * NC)
    row_spec = pl.BlockSpec((2, tc, C_W), lambda bb, tb: (bb, tb, 0))
    st_spec = pl.BlockSpec((2, npair, NC, 2 * NC), lambda bb, tb: (bb, 0, 0, 0))
    vt_spec = pl.BlockSpec((1, 2, npair, NC, tc), lambda bb, tb: (tb, bb, 0, 0, 0))
    vt_shape = jax.ShapeDtypeStruct((nt, batch, npair, NC, tc), F32)
    ot_e, ot_o, s_fin = pl.pallas_call(
        functools.partial(_rwkv_scan_kernel, tc=tc),
        grid=(batch // 2, nt),
        in_specs=[row_spec] * 5 + [vt_spec, vt_spec, st_spec],
        out_specs=[vt_spec, vt_spec, st_spec],
        out_shape=[vt_shape, vt_shape, jax.ShapeDtypeStruct((batch, npair, NC, 2 * NC), F32)],
        scratch_shapes=[pltpu.VMEM((2, npair, NC, 2 * NC), F32)],
        compiler_params=_cparams(("parallel", "arbitrary")),
        name="rwkv_scan",
    )(rows3(r), rows3(w), rows3(k), rows3(kk), rows3(b), vt[0], vt[1], s0p)
    o = jnp.stack([ot_e, ot_o], axis=0).transpose(2, 1, 5, 3, 0, 4).reshape(batch * seq, C_W)
    s_new = s_fin.reshape(batch, npair, NC, 2, NC).transpose(0, 1, 3, 2, 4).reshape(batch, H_C, NC, NC)
    return o, s_new


def _rwkv_post_kernel(o_ref, bonus_ref, g_ref, lng_ref, lnb_ref, ones_ref, out_ref):
    def seg_mean(x):
        return jnp.dot(x, ones_ref[...], precision=HIGHEST, preferred_element_type=F32) * (1.0 / NC)

    o = o_ref[...]
    oc = o - seg_mean(o)
    on = oc * lax.rsqrt(seg_mean(oc * oc) + GN_EPS_C)
    out_ref[...] = (on * lng_ref[...] + lnb_ref[...] + bonus_ref[...]) * g_ref[...]


def _rwkv_post(o, bonus, g, ln_g, ln_b, ones_bd):
    m = o.shape[0]
    tm = _pick_tile(m, 640)
    tok = pl.BlockSpec((tm, C_W), lambda i: (i, 0))
    vec = pl.BlockSpec((1, C_W), lambda i: (0, 0))
    return pl.pallas_call(
        _rwkv_post_kernel,
        grid=(m // tm,),
        in_specs=[tok, tok, tok, vec, vec, pl.BlockSpec((C_W, C_W), lambda i: (0, 0))],
        out_specs=tok,
        out_shape=jax.ShapeDtypeStruct((m, C_W), F32),
        compiler_params=_cparams(("parallel",)),
        name="rwkv_post",
    )(o, bonus, g, ln_g.reshape(1, C_W), ln_b.reshape(1, C_W), ones_bd)


def _perm_odd_columns():
    offs = np.cumsum([0, C_W, W_LORA, C_W, C_W, A_LORA, G_LORA])
    seg = lambda i: np.arange(offs[i], offs[i + 1])
    pc_perm = np.concatenate([seg(0), seg(2), seg(3), seg(1), seg(4), seg(5)])
    return pc_perm


def kernel(x_prompt, x_sample, cache_a_k, cache_a_v, state_b, state_c_wkv, state_c_shift, cache_d_k, cache_d_v,
           cache_d_logf, page_table, rel_bias, w_in_even, lam_params, subln_g, w_out_even, w_in_odd, c_mu, c_w0,
           c_w2, c_a0, c_a2, c_g2, c_kk, c_ka, c_rk, c_ln_g, c_ln_b, d_bf, w_out_odd, w_router, b_router, w_gate,
           w_up, w_down, ln1_g, ln1_b, ln2_g, ln2_b):
    batch, seq, d = x_prompt.shape
    dbatch, tdec, _ = x_sample.shape
    n_pages = page_table.shape[1]
    n_pool = cache_a_k.shape[1]
    past = n_pages * PAGE
    bt = batch * seq
    ds = dbatch * tdec
    pt_flat = page_table.reshape(-1).astype(jnp.int32)
    gp = _pick_tile(n_pages, PAGES_PER_STEP, 1)

    x = jnp.concatenate([x_prompt.reshape(bt, d), x_sample.reshape(ds, d)], axis=0)
    outs = {}

    def moe_block(x_in, layer):
        gates = _router(x_in, w_router, b_router)
        wgu = jnp.concatenate([w_gate[layer], w_up[layer]], axis=-1).astype(BF16)
        return _moe_ln(x_in, gates, wgu, w_down[layer].astype(BF16), ln2_g[layer], ln2_b[layer])

    lam_init0 = 0.8 - 0.6 * math.exp(-0.3 * 0)
    h0 = _proj(x, w_in_even[0].astype(BF16))
    ka_off = H_A * DKA
    va_off = 2 * H_A * DKA
    ka_all = h0[:, ka_off:ka_off + H_A * DKA]
    va_all = h0[:, va_off:va_off + H_A * DVA]
    outs["a_k_p"] = ka_all[:bt].reshape(1, batch, seq, H_A, DKA)
    outs["a_v_p"] = va_all[:bt].reshape(1, batch, seq, H_A, DVA)
    outs["a_k_s"] = ka_all[bt:].reshape(1, dbatch, tdec, H_A, DKA)
    outs["a_v_s"] = va_all[bt:].reshape(1, dbatch, tdec, H_A, DVA)

    oa_p = _diff_prompt(h0, batch, seq, rel_bias, lam_params[0], subln_g[0], lam_init0)
    qa_s = h0[bt:, 0:H_A * DKA].reshape(dbatch, tdec, H_A, DKA) * (DA ** -0.5)
    q_rows_a = _query_rows(qa_s, 2).astype(BF16)
    tab_d, tself_d = _diff_decode_tables(rel_bias, n_pages, gp, tdec)
    oa_s = _paged_attention(
        "diff", q_rows_a, cache_a_k.reshape(n_pool, PAGE * H_A, DKA), cache_a_v.reshape(n_pool, PAGE * H_A, DVA),
        pt_flat, n_pages, ka_all[bt:].reshape(dbatch, tdec, H_A * DKA), va_all[bt:].reshape(dbatch, tdec, H_A * DVA),
        (tab_d, tself_d, lam_params[0], subln_g[0].reshape(1, DVA)), 1.0, lam_init0)
    ob_p, sb_p = _retention(h0, 0, batch, seq, 0, jnp.zeros((batch, H_B * DKB, H_B * DVB), F32))
    ob_s, sb_s = _retention(h0, bt, dbatch, tdec, past, _state_to_bd(state_b[0].astype(F32)))
    outs["b_p"] = _bd_to_state(sb_p)[None]
    outs["b_s"] = _bd_to_state(sb_s)[None]
    mix_l = jnp.concatenate([oa_p, oa_s.reshape(ds, H_A * DVA)], axis=0)
    mix_r = jnp.concatenate([ob_p, ob_s], axis=0)
    x = _outproj_ln(mix_l, mix_r, x, w_out_even[0].astype(BF16), ln1_g[0], ln1_b[0])
    x = moe_block(x, 0)

    pc_perm = _perm_odd_columns()
    w_odd = w_in_odd[0]
    w_odd_p = jnp.concatenate([w_odd[:, pc_perm], w_odd[:, C_PROJ:],
                               jnp.zeros((d, ODD_IN_PAD - w_odd.shape[1]), w_odd.dtype)], axis=1).astype(BF16)
    h1 = _proj(x, w_odd_p)
    q_off = C_PROJ
    k_off = C_PROJ + D_W
    v_off = C_PROJ + 2 * D_W
    kd_all = h1[:, k_off:k_off + D_W]
    vd_all = h1[:, v_off:v_off + D_W]
    outs["d_k_p"] = kd_all[:bt].reshape(1, batch, seq, H_D, DD)
    outs["d_v_p"] = vd_all[:bt].reshape(1, batch, seq, H_D, DD)
    outs["d_k_s"] = kd_all[bt:].reshape(1, dbatch, tdec, H_D, DD)
    outs["d_v_s"] = vd_all[bt:].reshape(1, dbatch, tdec, H_D, DD)

    bf_pad = jnp.zeros((1, LANES), F32).at[0, :H_D].set(d_bf[0].astype(F32))
    lf_p, c_p = _fox_prep(h1, bf_pad, 0, batch, seq)
    lf_s, c_s = _fox_prep(h1, bf_pad, bt, dbatch, tdec)
    outs["d_lf_p"] = lf_p[:, :H_D].reshape(1, batch, seq, H_D)
    outs["d_lf_s"] = lf_s[:, :H_D].reshape(1, dbatch, tdec, H_D)
    od_p = _fox_prompt(h1, c_p, batch, seq)
    cn = c_s[:, :H_D].reshape(dbatch, tdec, H_D)
    cnt = cn.transpose(0, 2, 1)
    rho = cnt.reshape(dbatch, H_D * tdec, 1)
    logf_pool_t = cache_d_logf[0].astype(F32).transpose(0, 2, 1)
    kap = _suffix_sums(logf_pool_t, pt_flat, dbatch, n_pages)
    qd_s = h1[bt:, q_off:q_off + D_W].reshape(dbatch, tdec, H_D, DD)
    q_rows_d = _query_rows(qd_s, 1).astype(BF16)
    od_s = _paged_attention(
        "fox", q_rows_d, cache_d_k.reshape(n_pool, PAGE * H_D, DD), cache_d_v.reshape(n_pool, PAGE * H_D, DD),
        pt_flat, n_pages, kd_all[bt:].reshape(dbatch, tdec, D_W), vd_all[bt:].reshape(dbatch, tdec, D_W),
        (kap, rho, cnt), DD ** -0.5)

    pc_p = h1[:bt, :C_PROJ].reshape(batch, seq, C_PROJ)
    pc_s = h1[bt:, :C_PROJ].reshape(dbatch, tdec, C_PROJ)
    shift_p = jnp.zeros((batch, 1, C_PROJ), F32)
    shift_s = state_c_shift[0][:, pc_perm][:, None, :].astype(F32)
    prev = jnp.concatenate([
        jnp.concatenate([shift_p, pc_p[:, :-1]], axis=1).reshape(bt, C_PROJ),
        jnp.concatenate([shift_s, pc_s[:, :-1]], axis=1).reshape(ds, C_PROJ)], axis=0)
    inv_perm = np.argsort(pc_perm)
    outs["c_shift_p"] = pc_p[:, -1][:, inv_perm][None]
    outs["c_shift_s"] = pc_s[:, -1][:, inv_perm][None]
    w2p = jnp.concatenate([c_w2[0], jnp.zeros((A_LORA, C_W), F32)], axis=0).astype(BF16)
    a2p = jnp.concatenate([jnp.zeros((W_LORA, C_W), F32), c_a2[0]], axis=0).astype(BF16)
    ones_bd = jnp.asarray(np.kron(np.eye(H_C, dtype=np.float32), np.ones((NC, NC), np.float32)))
    r_, w_, k_, v_, kk_, b_, g_, bonus = _rwkv_prep(
        h1, prev, c_mu[0][pc_perm], c_w0[0], c_a0[0], c_kk[0], c_ka[0], c_rk[0].reshape(-1),
        w2p, a2p, c_g2[0].astype(BF16), ones_bd)
    o_p, sc_p = _rwkv_scan(r_[:bt], w_[:bt], k_[:bt], kk_[:bt], b_[:bt], v_[:bt],
                           jnp.zeros((batch, H_C, NC, NC), F32), batch, seq)
    o_s, sc_s = _rwkv_scan(r_[bt:], w_[bt:], k_[bt:], kk_[bt:], b_[bt:], v_[bt:],
                           state_c_wkv[0].astype(F32), dbatch, tdec)
    outs["c_wkv_p"] = sc_p[None]
    outs["c_wkv_s"] = sc_s[None]
    oc = _rwkv_post(jnp.concatenate([o_p, o_s], axis=0), bonus, g_, c_ln_g[0], c_ln_b[0], ones_bd)
    mix_r = jnp.concatenate([od_p, od_s.reshape(ds, D_W)], axis=0)
    x = _outproj_ln(oc, mix_r, x, w_out_odd[0].astype(BF16), ln1_g[1], ln1_b[1])
    x = moe_block(x, 1)

    y_prompt = x[:bt].reshape(batch, seq, d)
    y_sample = x[bt:].reshape(dbatch, tdec, d)
    return (y_prompt, y_sample,
            outs["a_k_p"], outs["a_v_p"], outs["a_k_s"], outs["a_v_s"],
            outs["b_p"], outs["b_s"],
            outs["c_wkv_p"], outs["c_wkv_s"], outs["c_shift_p"], outs["c_shift_s"],
            outs["d_k_p"], outs["d_v_p"], outs["d_lf_p"], outs["d_k_s"], outs["d_v_s"], outs["d_lf_s"])
```

```python
import functools
import math

import numpy as np
import jax
import jax.numpy as jnp
from jax import lax
from jax.experimental import pallas as pl
from jax.experimental.pallas import tpu as pltpu

F32 = jnp.float32
BF16 = jnp.bfloat16
HIGHEST = lax.Precision.HIGHEST

D_MODEL = 1024
DEPTH = 2
PAGE = 128
MIX_W = D_MODEL // 2
DA = 64
DKA = 2 * DA
DVA = 2 * DA
H_A = MIX_W // DVA
DKB = 64
DVB = 128
H_B = MIX_W // DVB
RET_CHUNK = 128
NC = 64
H_C = MIX_W // NC
W_LORA = 64
A_LORA = 64
G_LORA = 128
GN_EPS_C = 64e-5
DD = 128
H_D = MIX_W // DD
N_BUCKETS = 32
MAX_DISTANCE = 128
N_EXPERTS = 16
N_GROUPS = 4
EPG = N_EXPERTS // N_GROUPS
D_EXPERT = D_MODEL // 4
DN_ALPHA = (2 * DEPTH) ** 0.25
LN_EPS = 1e-5
RMS_EPS = 1e-5
C_W = H_C * NC
C_PROJ = C_W + W_LORA + C_W + C_W + A_LORA + G_LORA
D_W = H_D * DD
EVEN_IN = 2 * H_A * DKA + H_A * DVA + 2 * H_B * DKB + 2 * H_B * DVB
ODD_IN_PAD = C_PROJ + 3 * D_W + 128

LANES = 128
VMEM_LIMIT = 56 * 1024 * 1024
NEG = -1e30
ATT_TQ = 256
ATT_FAR_TILES = 4
FOX_TQ = 256
FOX_FAR_TILES = 4
PAGES_PER_STEP = 8
SCAN_TC = 8


def _cparams(sem):
    return pltpu.CompilerParams(dimension_semantics=sem, vmem_limit_bytes=VMEM_LIMIT)


def _pick_tile(m, cap, mult=8):
    best = None
    for t in range(mult, min(m, cap) + 1, mult):
        if m % t == 0:
            best = t
    assert best is not None, (m, cap, mult)
    return best


def _layer_norm_rows(z, g, b):
    mu = jnp.mean(z, axis=-1, keepdims=True)
    zc = z - mu
    var = jnp.mean(zc * zc, axis=-1, keepdims=True)
    return zc * lax.rsqrt(var + LN_EPS) * g + b


def _sigmoid(x):
    return 1.0 / (1.0 + jnp.exp(-x))


def _softplus(y):
    return jnp.maximum(y, 0.0) + jnp.log1p(jnp.exp(-jnp.abs(y)))


def _proj_kernel(x_ref, w_ref, o_ref):
    o_ref[...] = jnp.dot(x_ref[...].astype(BF16), w_ref[...], preferred_element_type=F32)


def _proj(x, w_bf):
    m, k = x.shape
    n = w_bf.shape[1]
    tm = _pick_tile(m, 320)
    return pl.pallas_call(
        _proj_kernel,
        grid=(m // tm,),
        in_specs=[pl.BlockSpec((tm, k), lambda i: (i, 0)),
                  pl.BlockSpec((k, n), lambda i: (0, 0))],
        out_specs=pl.BlockSpec((tm, n), lambda i: (i, 0)),
        out_shape=jax.ShapeDtypeStruct((m, n), F32),
        compiler_params=_cparams(("parallel",)),
        name="proj",
    )(x, w_bf)


def _outproj_kernel(a_ref, b_ref, x_ref, w_ref, g_ref, be_ref, o_ref):
    half = a_ref.shape[1]
    y = jnp.dot(a_ref[...].astype(BF16), w_ref[0:half, :], preferred_element_type=F32)
    y = y + jnp.dot(b_ref[...].astype(BF16), w_ref[half:2 * half, :], preferred_element_type=F32)
    z = DN_ALPHA * x_ref[...] + y
    o_ref[...] = _layer_norm_rows(z, g_ref[...], be_ref[...])


def _outproj_ln(mix_l, mix_r, x, w_bf, g, b):
    m, d = x.shape
    half = mix_l.shape[1]
    tm = _pick_tile(m, 640)
    return pl.pallas_call(
        _outproj_kernel,
        grid=(m // tm,),
        in_specs=[pl.BlockSpec((tm, half), lambda i: (i, 0)),
                  pl.BlockSpec((tm, half), lambda i: (i, 0)),
                  pl.BlockSpec((tm, d), lambda i: (i, 0)),
                  pl.BlockSpec((2 * half, d), lambda i: (0, 0)),
                  pl.BlockSpec((1, d), lambda i: (0, 0)),
                  pl.BlockSpec((1, d), lambda i: (0, 0))],
        out_specs=pl.BlockSpec((tm, d), lambda i: (i, 0)),
        out_shape=jax.ShapeDtypeStruct((m, d), F32),
        compiler_params=_cparams(("parallel",)),
        name="outproj_ln",
    )(mix_l, mix_r, x, w_bf, g.reshape(1, d), b.reshape(1, d))


def _router_kernel(x_ref, wr_ref, br_ref, g_ref):
    logits = _nt_dot(wr_ref[...], x_ref[...].astype(BF16))
    mx = jnp.max(logits, axis=0, keepdims=True)
    ex = jnp.exp(logits - mx)
    scores = ex / jnp.sum(ex, axis=0, keepdims=True)
    sel = scores + br_ref[...]
    rows = [sel[e:e + 1, :] for e in range(N_EXPERTS)]
    srow = [scores[e:e + 1, :] for e in range(N_EXPERTS)]
    grp = []
    for gi in range(N_GROUPS):
        a, b, c, d = rows[EPG * gi:EPG * gi + EPG]
        hi1, lo1 = jnp.maximum(a, b), jnp.minimum(a, b)
        hi2, lo2 = jnp.maximum(c, d), jnp.minimum(c, d)
        top1 = jnp.maximum(hi1, hi2)
        top2 = jnp.maximum(jnp.minimum(hi1, hi2), jnp.maximum(lo1, lo2))
        grp.append(top1 + top2)
    best_v = grp[0]
    best_i = jnp.zeros_like(best_v, dtype=jnp.int32)
    for gi in range(1, N_GROUPS):
        better = grp[gi] > best_v
        best_v = jnp.where(better, grp[gi], best_v)
        best_i = jnp.where(better, gi, best_i)
    chosen = []
    for e in range(N_EXPERTS):
        gi = e // EPG
        rank = jnp.zeros_like(best_i)
        for j in range(EPG * gi, EPG * gi + EPG):
            if j == e:
                continue
            beats = (rows[j] >= rows[e]) if j < e else (rows[j] > rows[e])
            rank = rank + beats.astype(jnp.int32)
        chosen.append((best_i == gi) & (rank < 2))
    wsum = jnp.zeros_like(best_v)
    for e in range(N_EXPERTS):
        wsum = wsum + jnp.where(chosen[e], srow[e], 0.0)
    for e in range(N_EXPERTS):
        g_ref[e:e + 1, :] = jnp.where(chosen[e], srow[e] / wsum, 0.0)


def _router(x, w_router, b_router):
    m, d = x.shape
    tm = _pick_tile(m, 1280, LANES)
    gt = pl.pallas_call(
        _router_kernel,
        grid=(m // tm,),
        in_specs=[pl.BlockSpec((tm, d), lambda i: (i, 0)),
                  pl.BlockSpec((N_EXPERTS, d), lambda i: (0, 0)),
                  pl.BlockSpec((N_EXPERTS, 1), lambda i: (0, 0))],
        out_specs=pl.BlockSpec((N_EXPERTS, tm), lambda i: (0, i)),
        out_shape=jax.ShapeDtypeStruct((N_EXPERTS, m), F32),
        compiler_params=_cparams(("parallel",)),
        name="router",
    )(x, w_router.T.astype(BF16), b_router.reshape(N_EXPERTS, 1))
    return gt.T


def _moe_kernel(x_ref, gate_ref, wgu_ref, wd_ref, g_ref, b_ref, o_ref, xb_ref, acc_ref):
    e = pl.program_id(1)

    @pl.when(e == 0)
    def _():
        xb_ref[...] = x_ref[...].astype(BF16)
        acc_ref[...] = jnp.zeros_like(acc_ref)

    h = jnp.dot(xb_ref[...], wgu_ref[0], preferred_element_type=F32)
    gates = gate_ref[...]
    lane = lax.broadcasted_iota(jnp.int32, gates.shape, 1)
    gcol = jnp.sum(jnp.where(lane == e, gates, 0.0), axis=1, keepdims=True)
    hg = h[:, :D_EXPERT]
    hu = h[:, D_EXPERT:]
    act = hg * _sigmoid(hg) * hu * gcol
    acc_ref[...] += jnp.dot(act.astype(BF16), wd_ref[0], preferred_element_type=F32)

    @pl.when(e == N_EXPERTS - 1)
    def _():
        z = DN_ALPHA * x_ref[...] + acc_ref[...]
        o_ref[...] = _layer_norm_rows(z, g_ref[...], b_ref[...])


def _moe_ln(x, gates, wgu_bf, wd_bf, g, b):
    m, d = x.shape
    tm = _pick_tile(m, 640)
    return pl.pallas_call(
        _moe_kernel,
        grid=(m // tm, N_EXPERTS),
        in_specs=[pl.BlockSpec((tm, d), lambda i, e: (i, 0)),
                  pl.BlockSpec((tm, N_EXPERTS), lambda i, e: (i, 0)),
                  pl.BlockSpec((1, d, 2 * D_EXPERT), lambda i, e: (e, 0, 0)),
                  pl.BlockSpec((1, D_EXPERT, d), lambda i, e: (e, 0, 0)),
                  pl.BlockSpec((1, d), lambda i, e: (0, 0)),
                  pl.BlockSpec((1, d), lambda i, e: (0, 0))],
        out_specs=pl.BlockSpec((tm, d), lambda i, e: (i, 0)),
        out_shape=jax.ShapeDtypeStruct((m, d), F32),
        scratch_shapes=[pltpu.VMEM((tm, d), BF16), pltpu.VMEM((tm, d), F32)],
        compiler_params=_cparams(("parallel", "arbitrary")),
        name="moe_ln",
    )(x, gates, wgu_bf, wd_bf, g.reshape(1, d), b.reshape(1, d))


def _softmax_update(s, v_bf, m_ref, l_ref, acc_ref, idx):
    m_prev = m_ref[idx][:, 0:1]
    l_prev = l_ref[idx][:, 0:1]
    m_new = jnp.maximum(m_prev, jnp.max(s, axis=1, keepdims=True))
    alpha = jnp.exp(m_prev - m_new)
    p = jnp.exp(s - m_new)
    l_new = alpha * l_prev + jnp.sum(p, axis=1, keepdims=True)
    acc_ref[idx] = alpha * acc_ref[idx] + jnp.dot(p.astype(BF16), v_bf, preferred_element_type=F32)
    m_ref[idx] = jnp.broadcast_to(m_new, m_ref.shape[1:])
    l_ref[idx] = jnp.broadcast_to(l_new, l_ref.shape[1:])


def _softmax_init(m_ref, l_ref, acc_ref):
    m_ref[...] = jnp.full(m_ref.shape, NEG, F32)
    l_ref[...] = jnp.zeros(l_ref.shape, F32)
    acc_ref[...] = jnp.zeros(acc_ref.shape, F32)


def _far_loops(nfar, body, far_tiles):
    nwide = nfar // far_tiles
    npair = nwide // 2

    def wide_pair(j, carry):
        body(2 * j, far_tiles)
        body(2 * j + 1, far_tiles)
        return carry

    def wide(j, carry):
        body(j, far_tiles)
        return carry

    def single(j, carry):
        body(j, 1)
        return carry

    lax.fori_loop(0, npair, wide_pair, 0)
    lax.fori_loop(2 * npair, nwide, wide, 0)
    lax.fori_loop(nwide * far_tiles, nfar, single, 0)


def _nt_dot(a, b):
    return lax.dot_general(a, b, (((1,), (1,)), ((), ())), preferred_element_type=F32)


def _tn_dot(a, b):
    return lax.dot_general(a, b, (((0,), (0,)), ((), ())), preferred_element_type=F32)


def _t5_bucket_np(dist):
    max_exact = N_BUCKETS // 2
    d = np.maximum(dist, 0)
    ratio = (np.maximum(d, 1).astype(np.float32) / np.float32(max_exact)).astype(np.float32)
    log_ratio = (np.log(ratio).astype(np.float32) / np.float32(math.log(MAX_DISTANCE / max_exact))).astype(np.float32)
    large = np.minimum(max_exact + (log_ratio * np.float32(N_BUCKETS - max_exact)).astype(np.int32), N_BUCKETS - 1)
    return np.where(d < max_exact, d, large).astype(np.int32)


_T5_FAR = int(np.min(np.nonzero(_t5_bucket_np(np.arange(4 * MAX_DISTANCE)) == N_BUCKETS - 1)[0]))
assert np.all(_t5_bucket_np(np.arange(_T5_FAR, 8 * MAX_DISTANCE)) == N_BUCKETS - 1)


def _lam_from_params(lp):
    s01 = jnp.sum(lp[0:1, :] * lp[1:2, :], axis=1, keepdims=True)
    s23 = jnp.sum(lp[2:3, :] * lp[3:4, :], axis=1, keepdims=True)
    return jnp.exp(s01) - jnp.exp(s23)


def _diff_finalize(o0, o1, lam, gain, lam_init):
    o = o0 - lam * o1
    ms = jnp.mean(o * o, axis=-1, keepdims=True)
    return o * lax.rsqrt(ms + RMS_EPS) * gain * (1.0 - lam_init)


def _diff_prompt_kernel(q_ref, k_ref, v_ref, tab_ref, lp_ref, gain_ref, o_ref,
                        kb_ref, vb_ref, m_ref, l_ref, acc_ref, *, tq, lam_init):
    i = pl.program_id(2)

    @pl.when(i == 0)
    def _():
        kb_ref[...] = k_ref[...].astype(BF16)
        vb_ref[...] = v_ref[...].astype(BF16)

    _softmax_init(m_ref, l_ref, acc_ref)
    qf = q_ref[...] * (DA ** -0.5)
    lane = lax.broadcasted_iota(jnp.int32, qf.shape, 1)
    qm = (jnp.where(lane < DA, qf, 0.0).astype(BF16), jnp.where(lane >= DA, qf, 0.0).astype(BF16))

    def chunk(j, bias, tiles=1):
        width = tiles * tq
        ks = pl.multiple_of(j * width, width)
        kc = kb_ref[pl.ds(ks, width), :]
        vc = vb_ref[pl.ds(ks, width), :]
        for mi in range(2):
            s = _nt_dot(qm[mi], kc)
            if bias is not None:
                s = s + bias
            _softmax_update(s, vc, m_ref, l_ref, acc_ref, mi)

    _far_loops(jnp.maximum(i - 1, 0), lambda j, tiles: chunk(j, None, tiles), ATT_FAR_TILES)

    @pl.when(i >= 1)
    def _():
        chunk(i - 1, tab_ref[0, 1])

    chunk(i, tab_ref[0, 0])
    lam = _lam_from_params(lp_ref[...]) + lam_init
    o0 = acc_ref[0] / l_ref[0][:, 0:1]
    o1 = acc_ref[1] / l_ref[1][:, 0:1]
    o_ref[...] = _diff_finalize(o0, o1, lam, gain_ref[...], lam_init)


def _diff_tables(rel_bias, tq):
    qi = np.arange(tq)[:, None]
    kj = np.arange(tq)[None, :]
    rb = rel_bias.astype(F32)
    rel = rb - rb[N_BUCKETS - 1][None, :]
    def lookup(bucket):
        onehot = jnp.asarray(bucket)[..., None] == jnp.arange(N_BUCKETS)
        return jnp.sum(jnp.where(onehot[..., None], rel[None, None], 0.0), axis=2)

    diag = jnp.where(jnp.asarray(qi >= kj)[..., None], lookup(_t5_bucket_np(qi - kj)), NEG)
    prev = lookup(_t5_bucket_np(tq + qi - kj))
    return jnp.moveaxis(jnp.stack([diag, prev], 0), -1, 0)


def _diff_prompt(h0, batch, seq, rel_bias, lam_params, subln_g, lam_init):
    tq = min(ATT_TQ, seq)
    assert seq % tq == 0 and tq + 1 >= _T5_FAR
    nq = seq // tq
    tab = _diff_tables(rel_bias, tq)
    kern = functools.partial(_diff_prompt_kernel, tq=tq, lam_init=lam_init)
    qoff, koff, voff = 0, H_A, 2 * H_A
    return pl.pallas_call(
        kern,
        grid=(batch, H_A, nq),
        in_specs=[pl.BlockSpec((tq, DKA), lambda b, h, i: (b * nq + i, qoff + h)),
                  pl.BlockSpec((seq, DKA), lambda b, h, i: (b, koff + h)),
                  pl.BlockSpec((seq, DVA), lambda b, h, i: (b, voff + h)),
                  pl.BlockSpec((1, 2, tq, tq), lambda b, h, i: (h, 0, 0, 0)),
                  pl.BlockSpec((4, DA), lambda b, h, i: (0, 0)),
                  pl.BlockSpec((1, DVA), lambda b, h, i: (0, 0))],
        out_specs=pl.BlockSpec((tq, DVA), lambda b, h, i: (b * nq + i, h)),
        out_shape=jax.ShapeDtypeStruct((batch * seq, H_A * DVA), F32),
        scratch_shapes=[pltpu.VMEM((seq, DKA), BF16), pltpu.VMEM((seq, DVA), BF16),
                        pltpu.VMEM((2, tq, LANES), F32), pltpu.VMEM((2, tq, LANES), F32),
                        pltpu.VMEM((2, tq, DVA), F32)],
        compiler_params=_cparams(("parallel", "parallel", "arbitrary")),
        name="diff_prompt",
    )(h0, h0, h0, tab, lam_params, subln_g.reshape(1, DVA))


def _fox_prompt_kernel(q_ref, k_ref, v_ref, c_ref, kap_ref, o_ref,
                       kb_ref, vb_ref, m_ref, l_ref, acc_ref, *, tq):
    h = pl.program_id(1)
    i = pl.program_id(2)

    @pl.when(i == 0)
    def _():
        kb_ref[...] = k_ref[...].astype(BF16)
        vb_ref[...] = v_ref[...].astype(BF16)

    _softmax_init(m_ref, l_ref, acc_ref)
    qb = q_ref[...].astype(BF16)
    cblk = c_ref[...]
    lane = lax.broadcasted_iota(jnp.int32, cblk.shape, 1)
    rho = jnp.sum(jnp.where(lane == h, cblk, 0.0), axis=1, keepdims=True)

    def chunk(j, masked, tiles=1):
        width = tiles * tq
        ks = pl.multiple_of(j * width, width)
        kc = kb_ref[pl.ds(ks, width), :]
        vc = vb_ref[pl.ds(ks, width), :]
        s = _nt_dot(qb, kc) * (DD ** -0.5) + (rho - kap_ref[0, :, pl.ds(ks, width)])
        if masked:
            r = lax.broadcasted_iota(jnp.int32, s.shape, 0)
            c = lax.broadcasted_iota(jnp.int32, s.shape, 1)
            s = jnp.where(r >= c, s, NEG)
        _softmax_update(s, vc, m_ref, l_ref, acc_ref, 0)

    _far_loops(i, lambda j, tiles: chunk(j, False, tiles), FOX_FAR_TILES)
    chunk(i, True)
    o_ref[...] = acc_ref[0] / l_ref[0][:, 0:1]


def _fox_prompt(h1, c_all, batch, seq):
    tq = min(FOX_TQ, seq)
    assert seq % tq == 0
    nq = seq // tq
    qoff = C_PROJ // LANES
    koff = qoff + H_D
    voff = koff + H_D
    kap = c_all[:batch * seq, :H_D].reshape(batch, seq, H_D).transpose(0, 2, 1).reshape(batch * H_D, 1, seq)
    kern = functools.partial(_fox_prompt_kernel, tq=tq)
    return pl.pallas_call(
        kern,
        grid=(batch, H_D, nq),
        in_specs=[pl.BlockSpec((tq, DD), lambda b, h, i: (b * nq + i, qoff + h)),
                  pl.BlockSpec((seq, DD), lambda b, h, i: (b, koff + h)),
                  pl.BlockSpec((seq, DD), lambda b, h, i: (b, voff + h)),
                  pl.BlockSpec((tq, LANES), lambda b, h, i: (b * nq + i, 0)),
                  pl.BlockSpec((1, 1, seq), lambda b, h, i: (b * H_D + h, 0, 0))],
        out_specs=pl.BlockSpec((tq, DD), lambda b, h, i: (b * nq + i, h)),
        out_shape=jax.ShapeDtypeStruct((batch * seq, D_W), F32),
        scratch_shapes=[pltpu.VMEM((seq, DD), BF16), pltpu.VMEM((seq, DD), BF16),
                        pltpu.VMEM((1, tq, LANES), F32), pltpu.VMEM((1, tq, LANES), F32),
                        pltpu.VMEM((1, tq, DD), F32)],
        compiler_params=_cparams(("parallel", "parallel", "arbitrary")),
        name="fox_prompt",
    )(h1, h1, h1, c_all, kap)


def _fox_prep_kernel(fd_ref, bf_ref, tri_ref, lf_ref, c_ref, carry_ref):
    @pl.when(pl.program_id(1) == 0)
    def _():
        carry_ref[...] = jnp.zeros_like(carry_ref)

    x = fd_ref[...] + bf_ref[...]
    lf = -_softplus(-x)
    inc = jnp.dot(tri_ref[...], lf, precision=HIGHEST, preferred_element_type=F32) + carry_ref[0:1, :]
    lf_ref[...] = lf
    c_ref[...] = inc
    tm = inc.shape[0]
    carry_ref[...] = jnp.broadcast_to(inc[tm - 1:tm, :], carry_ref.shape)


def _fox_prep(h1, bf_pad, row0, batch, seq):
    tm = _pick_tile(seq, 256)
    nt = seq // tm
    assert row0 % tm == 0
    blk0 = row0 // tm
    col = (ODD_IN_PAD - LANES) // LANES
    tri = jnp.asarray(np.tril(np.ones((tm, tm), np.float32)))
    return pl.pallas_call(
        _fox_prep_kernel,
        grid=(batch, nt),
        in_specs=[pl.BlockSpec((tm, LANES), lambda b, i: (blk0 + b * nt + i, col)),
                  pl.BlockSpec((1, LANES), lambda b, i: (0, 0)),
                  pl.BlockSpec((tm, tm), lambda b, i: (0, 0))],
        out_specs=[pl.BlockSpec((tm, LANES), lambda b, i: (b * nt + i, 0)),
                   pl.BlockSpec((tm, LANES), lambda b, i: (b * nt + i, 0))],
        out_shape=[jax.ShapeDtypeStruct((batch * seq, LANES), F32),
                   jax.ShapeDtypeStruct((batch * seq, LANES), F32)],
        scratch_shapes=[pltpu.VMEM((8, LANES), F32)],
        compiler_params=_cparams(("parallel", "arbitrary")),
        name="fox_prep",
    )(h1, bf_pad, tri)


def _suffix_kernel(pt_ref, *refs, gpages):
    lf_refs = refs[:gpages]
    ut_ref = refs[gpages]
    o_ref = refs[gpages + 1]
    carry_ref = refs[gpages + 2]

    @pl.when(pl.program_id(1) == 0)
    def _():
        carry_ref[...] = jnp.zeros_like(carry_ref)

    run = carry_ref[:, 0:1]
    for j in reversed(range(gpages)):
        lf = lf_refs[j][...]
        incl = jnp.dot(lf, ut_ref[...], precision=HIGHEST, preferred_element_type=F32)
        o_ref[0, :, j * PAGE:(j + 1) * PAGE] = incl - lf + run
        run = run + incl[:, 0:1]
    carry_ref[...] = jnp.broadcast_to(run, carry_ref.shape)


def _suffix_sums(logf_pool_t, page_table_flat, dbatch, n_pages):
    gp = _pick_tile(n_pages, PAGES_PER_STEP, 1)
    ng = n_pages // gp
    ut = jnp.asarray(np.tril(np.ones((PAGE, PAGE), np.float32)))

    def page_map(j):
        return lambda b, g, pt: (pt[b * n_pages + (ng - 1 - g) * gp + j], 0, 0)

    in_specs = [pl.BlockSpec((None, H_D, PAGE), page_map(j)) for j in range(gp)]
    in_specs.append(pl.BlockSpec((PAGE, PAGE), lambda b, g, pt: (0, 0)))
    return pl.pallas_call(
        functools.partial(_suffix_kernel, gpages=gp),
        grid_spec=pltpu.PrefetchScalarGridSpec(
            num_scalar_prefetch=1,
            grid=(dbatch, ng),
            in_specs=in_specs,
            out_specs=pl.BlockSpec((1, H_D, gp * PAGE), lambda b, g, pt: (b, 0, ng - 1 - g)),
            scratch_shapes=[pltpu.VMEM((H_D, LANES), F32)]),
        out_shape=jax.ShapeDtypeStruct((dbatch, H_D, n_pages * PAGE), F32),
        compiler_params=_cparams(("parallel", "arbitrary")),
        name="fox_suffix",
    )(page_table_flat, *([logf_pool_t] * gp), ut)


def _paged_kernel(pt_ref, *refs, gpages, tdec, mode, scale, lam_init, chunk, nheads):
    q_ref = refs[0]
    k_refs = refs[1:1 + gpages]
    v_refs = refs[1 + gpages:1 + 2 * gpages]
    rest = refs[1 + 2 * gpages:]
    if mode == "diff":
        tab_ref, tself_ref, kn_ref, vn_ref, lp_ref, gain_ref, o_ref, s_ref, vb_ref = rest
    else:
        kap_ref, rho_ref, cnt_ref, kn_ref, vn_ref, o_ref, s_ref, vb_ref = rest
    g = pl.program_id(1)
    ng = pl.num_programs(1)
    gw = gpages * PAGE
    rows, past = s_ref.shape
    rph = rows // nheads
    dh = q_ref.shape[2]
    off = pl.multiple_of(g * gw, gw)

    def head_rows(ref, h):
        return ref[pl.ds(h, PAGE, stride=nheads), :]

    for h in range(nheads):
        hs = slice(h * rph, (h + 1) * rph)
        kh = jnp.concatenate([head_rows(r, h).astype(BF16) for r in k_refs], axis=0)
        s = _nt_dot(q_ref[0, hs, :], kh)
        if scale != 1.0:
            s = s * scale
        if mode == "diff":
            s = s + tab_ref[0, hs, :]
        else:
            s = (s + rho_ref[0, hs, :]) + kap_ref[0, h:h + 1, :]
        s_ref[hs, pl.ds(off, gw)] = s
        for j in range(gpages):
            vb_ref[h, pl.ds(off + j * PAGE, PAGE), :] = head_rows(v_refs[j], h).astype(BF16)

    @pl.when(g == ng - 1)
    def _():
        tails = []
        for h in range(nheads):
            hs = slice(h * rph, (h + 1) * rph)
            kn = kn_ref[0, :, h * dh:(h + 1) * dh].astype(BF16)
            t = _nt_dot(q_ref[0, hs, :], kn)
            if scale != 1.0:
                t = t * scale
            if mode == "diff":
                t = t + tself_ref[hs, :]
            else:
                t = (t + rho_ref[0, hs, :]) - cnt_ref[0, h:h + 1, :]
                r = lax.broadcasted_iota(jnp.int32, t.shape, 0)
                c = lax.broadcasted_iota(jnp.int32, t.shape, 1)
                t = jnp.where(c <= r, t, NEG)
            tails.append(t)
        tail = jnp.concatenate(tails, axis=0)
        if mode == "diff":
            lam = _lam_from_params(lp_ref[...]) + lam_init
        nchunk = past // chunk

        def sc(ci, hs=slice(None)):
            return s_ref[hs, pl.ds(pl.multiple_of(ci * chunk, chunk), chunk)]

        m = lax.fori_loop(0, nchunk, lambda ci, mm: jnp.maximum(mm, jnp.max(sc(ci), axis=1, keepdims=True)),
                          jnp.max(tail, axis=1, keepdims=True))
        den = lax.fori_loop(0, nchunk, lambda ci, dd: dd + jnp.sum(jnp.exp(sc(ci) - m), axis=1, keepdims=True),
                            jnp.sum(jnp.exp(tail - m), axis=1, keepdims=True))

        def weights(sv, h):
            hs = slice(h * rph, (h + 1) * rph)
            p = jnp.exp(sv - m[hs]) / den[hs]
            if mode == "diff":
                p = p[0:tdec] - lam * p[tdec:2 * tdec]
            return p.astype(BF16)

        def pv_body(ci, accs):
            cs = pl.multiple_of(ci * chunk, chunk)
            return tuple(
                accs[h] + jnp.dot(weights(sc(ci, slice(h * rph, (h + 1) * rph)), h), vb_ref[h, pl.ds(cs, chunk), :],
                                  preferred_element_type=F32)
                for h in range(nheads))

        init = tuple(
            jnp.dot(weights(tails[h], h), vn_ref[0, :, h * dh:(h + 1) * dh].astype(BF16), preferred_element_type=F32)
            for h in range(nheads))
        outs = lax.fori_loop(0, nchunk, pv_body, init)
        for h in range(nheads):
            oh = outs[h]
            if mode == "diff":
                ms = jnp.mean(oh * oh, axis=-1, keepdims=True)
                oh = oh * lax.rsqrt(ms + RMS_EPS) * gain_ref[...] * (1.0 - lam_init)
            o_ref[0, :, h * dh:(h + 1) * dh] = oh


def _query_rows(q, n_maps):
    db, t, nh, w = q.shape
    dm = w // n_maps
    qh = q.transpose(0, 2, 1, 3)
    sel = jnp.asarray(np.kron(np.eye(n_maps, dtype=np.float32), np.ones((1, dm), np.float32)))
    rows = qh[:, :, None, :, :] * sel[None, None, :, None, :]
    return rows.reshape(db, nh * n_maps * t, w)


def _paged_attention(mode, q_rows_bf, k_pool, v_pool, page_table_flat, n_pages, k_new, v_new, extra, scale,
                     lam_init=0.0):
    dbatch, rows, dh = q_rows_bf.shape
    tdec = k_new.shape[1]
    width = k_new.shape[2]
    nheads = width // dh
    gp = _pick_tile(n_pages, PAGES_PER_STEP, 1)
    ng = n_pages // gp
    past = n_pages * PAGE
    chunk = _pick_tile(past, 2048, LANES)
    pad_rows = lambda x: jnp.pad(x, ((0, 0), (0, PAGE - tdec), (0, 0)))
    k_new = pad_rows(k_new)
    v_new = pad_rows(v_new)

    def page_map(j):
        return lambda b, g, pt: (pt[b * n_pages + g * gp + j], 0, 0)

    in_specs = [pl.BlockSpec((1, rows, dh), lambda b, g, pt: (b, 0, 0))]
    in_specs += [pl.BlockSpec((None, PAGE * nheads, dh), page_map(j)) for j in range(gp)]
    in_specs += [pl.BlockSpec((None, PAGE * nheads, dh), page_map(j)) for j in range(gp)]
    new_spec = pl.BlockSpec((1, PAGE, width), lambda b, g, pt: (b, 0, 0))
    if mode == "diff":
        tab, tself, lam_params, gain = extra
        tself = jnp.pad(tself, ((0, 0), (0, PAGE - tdec)), constant_values=NEG)
        in_specs += [pl.BlockSpec((1, rows, gp * PAGE), lambda b, g, pt: ((g + 1) // ng, 0, 0)),
                     pl.BlockSpec((rows, PAGE), lambda b, g, pt: (0, 0)),
                     new_spec, new_spec,
                     pl.BlockSpec((4, DA), lambda b, g, pt: (0, 0)),
                     pl.BlockSpec((1, DVA), lambda b, g, pt: (0, 0))]
        operands = [tab, tself, k_new, v_new, lam_params, gain]
    else:
        kap, rho, cnt = extra
        cnt = jnp.pad(cnt, ((0, 0), (0, 0), (0, PAGE - tdec)))
        in_specs += [pl.BlockSpec((1, H_D, gp * PAGE), lambda b, g, pt: (b, 0, g)),
                     pl.BlockSpec((1, rows, 1), lambda b, g, pt: (b, 0, 0)),
                     pl.BlockSpec((1, H_D, PAGE), lambda b, g, pt: (b, 0, 0)),
                     new_spec, new_spec]
        operands = [kap, rho, cnt, k_new, v_new]
    kern = functools.partial(_paged_kernel, gpages=gp, tdec=tdec, mode=mode, scale=scale, lam_init=lam_init,
                             chunk=chunk, nheads=nheads)
    return pl.pallas_call(
        kern,
        grid_spec=pltpu.PrefetchScalarGridSpec(
            num_scalar_prefetch=1,
            grid=(dbatch, ng),
            in_specs=in_specs,
            out_specs=pl.BlockSpec((1, tdec, width), lambda b, g, pt: (b, 0, 0)),
            scratch_shapes=[pltpu.VMEM((rows, past), F32), pltpu.VMEM((nheads, past, dh), BF16)]),
        out_shape=jax.ShapeDtypeStruct((dbatch, tdec, width), F32),
        compiler_params=_cparams(("parallel", "arbitrary")),
        name="paged_" + mode,
    )(page_table_flat, q_rows_bf, *([k_pool] * gp), *([v_pool] * gp), *operands)


def _diff_decode_tables(rel_bias, n_pages, gp, tdec):
    rb = rel_bias.astype(F32)
    rel = (rb - rb[N_BUCKETS - 1][None, :]).T
    past = n_pages * PAGE
    t = np.arange(tdec)[:, None]
    kpos = past - gp * PAGE + np.arange(gp * PAGE)[None, :]
    assert PAGE + 1 >= _T5_FAR
    last = rel[:, _t5_bucket_np(past + t - kpos)]
    s = np.arange(tdec)[None, :]
    selfb = jnp.where(jnp.asarray(t >= s)[None], rel[:, _t5_bucket_np(t - s)], NEG)
    rep = lambda x: jnp.broadcast_to(x[:, None], (H_A, 2) + x.shape[1:]).reshape((H_A * 2 * tdec,) + x.shape[2:])
    last_r = rep(last)
    tab = jnp.stack([jnp.zeros_like(last_r), last_r], 0)
    return tab, rep(selfb)


def _retention_kernel(q_ref, k_ref, v_ref, gb_ref, cos_ref, sin_ref, qdec_ref, kdec_ref, dmat_ref,
                      gc_ref, bd_ref, s0_ref, o_ref, sout_ref, s_ref):
    c = pl.program_id(1)

    @pl.when(c == 0)
    def _():
        s_ref[...] = s0_ref[0]

    cos = cos_ref[...]
    sin = sin_ref[...]
    width = cos.shape[1]
    lane = lax.broadcasted_iota(jnp.int32, cos.shape, 1)
    first_half = (lane % DKB) < (DKB // 2)

    def rot(x):
        swapped = jnp.where(first_half, -pltpu.roll(x, width - DKB // 2, 1), pltpu.roll(x, DKB // 2, 1))
        return x * cos + swapped * sin

    qr = rot(q_ref[...])
    kr = rot(k_ref[...]) * (DKB ** -0.5)
    v = v_ref[...]
    vb = v.astype(BF16)
    state = s_ref[...]
    cross = jnp.dot((qr * qdec_ref[...]).astype(BF16), state.astype(BF16), preferred_element_type=F32)
    lane_pair = lax.broadcasted_iota(jnp.int32, (qr.shape[0], LANES), 1)
    gate = gb_ref[...]
    for h in range(H_B):
        cb = (h * DKB) // LANES
        lo = (h * DKB) % LANES
        qh = qr[:, cb * LANES:(cb + 1) * LANES]
        qh = jnp.where((lane_pair >= lo) & (lane_pair < lo + DKB), qh, 0.0).astype(BF16)
        kh = kr[:, cb * LANES:(cb + 1) * LANES].astype(BF16)
        att = _nt_dot(qh, kh) * dmat_ref[h]
        oh = jnp.dot(att.astype(BF16), vb[:, h * DVB:(h + 1) * DVB], preferred_element_type=F32)
        oh = oh + cross[:, h * DVB:(h + 1) * DVB]
        ms = jnp.mean(oh * oh, axis=-1, keepdims=True)
        gh = gate[:, h * DVB:(h + 1) * DVB]
        o_ref[:, h * DVB:(h + 1) * DVB] = oh * lax.rsqrt(ms + RMS_EPS) * (gh * _sigmoid(gh))
    upd = _tn_dot((kr * kdec_ref[...]).astype(BF16), vb)
    new_state = state * gc_ref[...] + upd * bd_ref[...]
    s_ref[...] = new_state

    @pl.when(c == pl.num_programs(1) - 1)
    def _():
        sout_ref[0] = new_state


def _retention(h0, row0, batch, seq, start, s0_bd):
    ch = RET_CHUNK if seq % RET_CHUNK == 0 else seq
    n = seq // ch
    assert row0 % ch == 0
    blk0 = row0 // ch
    kw = H_B * DKB
    vw = H_B * DVB
    qcol = (2 * H_A * DKA + H_A * DVA) // kw
    kcol = qcol + 1
    vcol = (2 * H_A * DKA + H_A * DVA + 2 * kw) // vw
    gcol = vcol + 1
    half = DKB // 2
    inv = 1.0 / (10000.0 ** (jnp.arange(half, dtype=F32) / half))
    pos = (start + jnp.arange(seq)).astype(F32)
    ang = pos[:, None] * inv[None, :]
    cos = jnp.tile(jnp.cos(ang), (1, 2 * H_B))
    sin = jnp.tile(jnp.sin(ang), (1, 2 * H_B))
    log_g = jnp.log(1.0 - jnp.exp2(-5.0 - jnp.arange(H_B, dtype=F32)))
    idx = jnp.arange(ch, dtype=F32)
    diff = idx[:, None] - idx[None, :]
    dmat = jnp.where(diff >= 0, jnp.exp(log_g[:, None, None] * jnp.maximum(diff, 0.0)), 0.0)
    qdec = jnp.repeat(jnp.exp(log_g[None, :] * (idx[:, None] + 1.0)), DKB, axis=1)
    kdec = jnp.repeat(jnp.exp(log_g[None, :] * (ch - 1.0 - idx[:, None])), DKB, axis=1)
    gc = jnp.broadcast_to(jnp.repeat(jnp.exp(log_g * ch), DKB)[:, None], (kw, vw))
    bd = jnp.asarray(np.kron(np.eye(H_B, dtype=np.float32), np.ones((DKB, DVB), np.float32)))
    out, s_fin = pl.pallas_call(
        _retention_kernel,
        grid=(batch, n),
        in_specs=[pl.BlockSpec((ch, kw), lambda b, c: (blk0 + b * n + c, qcol)),
                  pl.BlockSpec((ch, kw), lambda b, c: (blk0 + b * n + c, kcol)),
                  pl.BlockSpec((ch, vw), lambda b, c: (blk0 + b * n + c, vcol)),
                  pl.BlockSpec((ch, vw), lambda b, c: (blk0 + b * n + c, gcol)),
                  pl.BlockSpec((ch, kw), lambda b, c: (c, 0)),
                  pl.BlockSpec((ch, kw), lambda b, c: (c, 0)),
                  pl.BlockSpec((ch, kw), lambda b, c: (0, 0)),
                  pl.BlockSpec((ch, kw), lambda b, c: (0, 0)),
                  pl.BlockSpec((H_B, ch, ch), lambda b, c: (0, 0, 0)),
                  pl.BlockSpec((kw, vw), lambda b, c: (0, 0)),
                  pl.BlockSpec((kw, vw), lambda b, c: (0, 0)),
                  pl.BlockSpec((1, kw, vw), lambda b, c: (b, 0, 0))],
        out_specs=[pl.BlockSpec((ch, vw), lambda b, c: (b * n + c, 0)),
                   pl.BlockSpec((1, kw, vw), lambda b, c: (b, 0, 0))],
        out_shape=[jax.ShapeDtypeStruct((batch * seq, vw), F32),
                   jax.ShapeDtypeStruct((batch, kw, vw), F32)],
        scratch_shapes=[pltpu.VMEM((kw, vw), F32)],
        compiler_params=_cparams(("parallel", "arbitrary")),
        name="retention",
    )(h0, h0, h0, h0, cos, sin, qdec, kdec, dmat, gc, bd, s0_bd)
    return out, s_fin


def _state_to_bd(s):
    b = s.shape[0]
    eye = jnp.eye(H_B, dtype=s.dtype)
    return (s[:, :, :, None, :] * eye[None, :, None, :, None]).reshape(b, H_B * DKB, H_B * DVB)


def _bd_to_state(sbd):
    b = sbd.shape[0]
    s5 = sbd.reshape(b, H_B, DKB, H_B, DVB)
    return jnp.stack([s5[:, h, :, h, :] for h in range(H_B)], axis=1)


def _rwkv_prep_kernel(pc_ref, prev_ref, mu_ref, w0_ref, a0_ref, kkw_ref, kaw_ref, rk_ref,
                      w2_ref, a2_ref, g2_ref, ones_ref,
                      r_ref, w_ref, k_ref, v_ref, kk_ref, b_ref, g_ref, bonus_ref):
    pc = pc_ref[...]
    pm = pc + (prev_ref[...] - pc) * mu_ref[...]
    r = pm[:, 0:C_W]
    k = pm[:, C_W:2 * C_W]
    v = pm[:, 2 * C_W:3 * C_W]
    wa = pm[:, 3 * C_W:3 * C_W + W_LORA + A_LORA]
    gd = pm[:, 3 * C_W + W_LORA + A_LORA:]
    wl = jnp.dot(jnp.tanh(wa).astype(BF16), w2_ref[...], preferred_element_type=F32)
    w_log = -_softplus(-(w0_ref[...] + wl)) - 0.5
    decay = jnp.exp(-jnp.exp(w_log))
    a = _sigmoid(a0_ref[...] + jnp.dot(wa.astype(BF16), a2_ref[...], preferred_element_type=F32))
    g = jnp.dot(_sigmoid(gd).astype(BF16), g2_ref[...], preferred_element_type=F32)

    def seg_sum(x):
        return jnp.dot(x, ones_ref[...], precision=HIGHEST, preferred_element_type=F32)

    kkr = k * kkw_ref[...]
    kk = kkr / jnp.maximum(jnp.sqrt(seg_sum(kkr * kkr)), 1e-12)
    k2 = k * (1.0 + (a - 1.0) * kaw_ref[...])
    r_ref[...] = r
    w_ref[...] = decay
    k_ref[...] = k2
    v_ref[...] = v
    kk_ref[...] = kk
    b_ref[...] = kk * a
    g_ref[...] = g
    bonus_ref[...] = seg_sum(r * k2 * rk_ref[...]) * v


def _rwkv_prep(h1, prev, mu_p, w0, a0, kkw, kaw, rk, w2p, a2p, g2, ones_bd):
    m = h1.shape[0]
    tm = _pick_tile(m, 320)
    row = lambda x: x.reshape(1, -1)
    tok = lambda w: pl.BlockSpec((tm, w), lambda i: (i, 0))
    full = lambda a: pl.BlockSpec(a.shape, lambda i: (0, 0))
    small = [row(mu_p), row(w0), row(a0), row(kkw), row(kaw), row(rk), w2p, a2p, g2, ones_bd]
    outs = pl.pallas_call(
        _rwkv_prep_kernel,
        grid=(m // tm,),
        in_specs=[tok(C_PROJ), tok(C_PROJ)] + [full(a) for a in small],
        out_specs=[tok(C_W)] * 8,
        out_shape=[jax.ShapeDtypeStruct((m, C_W), F32)] * 8,
        compiler_params=_cparams(("parallel",)),
        name="rwkv_prep",
    )(h1, prev, *small)
    return outs


def _split_bf16(x):
    hi = x.astype(BF16)
    return hi, (x - hi.astype(F32)).astype(BF16)


def _dot3(a, b, dot, axis):
    ah, al = _split_bf16(a)
    bh, bl = _split_bf16(b)
    return dot(jnp.concatenate([ah, ah, al], axis=axis), jnp.concatenate([bh, bl, bh], axis=axis))


def _rwkv_scan_kernel(r_ref, w_ref, k_ref, kk_ref, b_ref, v_ref, s0_ref, o_ref, sout_ref, s_ref, *, tc):
    tb = pl.program_id(1)

    @pl.when(tb == 0)
    def _():
        s_ref[...] = s0_ref[...]

    nb, npair = s_ref.shape[0], s_ref.shape[1]
    pairs = [(bi, p) for bi in range(nb) for p in range(npair)]
    g = len(pairs)
    pw = 2 * NC
    shape = (g, NC, pw)
    lo = lax.broadcasted_iota(jnp.int32, shape, 2) < NC
    lo_rows = lax.broadcasted_iota(jnp.int32, (tc, pw), 1) < NC
    step_of_row = lax.broadcasted_iota(jnp.int32, (2 * tc, pw), 0) % tc

    def rows(ref, t):
        return jnp.concatenate([ref[bi, t:t + 1, p * pw:(p + 1) * pw][None] for bi, p in pairs], axis=0)

    def half_sums(x):
        even = jnp.sum(jnp.where(lo, x, 0.0), axis=2, keepdims=True)
        odd = jnp.sum(jnp.where(lo, 0.0, x), axis=2, keepdims=True)
        return even, odd

    def parity_rows(x):
        return jnp.concatenate([jnp.where(lo_rows, x, 0.0), jnp.where(lo_rows, 0.0, x)], axis=0)

    vk, rm = [], []
    for bi, p in pairs:
        sl = slice(p * pw, (p + 1) * pw)
        kp = parity_rows(k_ref[bi, :, sl])
        kb = jnp.concatenate([jnp.where(step_of_row == t, kp, 0.0) for t in range(tc)], axis=1)
        v8 = v_ref[bi, :, sl]
        va = jnp.concatenate([v8[:, 0:NC], v8[:, NC:pw]], axis=0)
        vk.append(_dot3(va, kb, _tn_dot, 0))
        rm.append(parity_rows(r_ref[bi, :, sl]))

    state = s_ref[...].reshape(shape)
    acc = [jnp.zeros((2 * tc, NC), F32) for _ in pairs]
    for t in range(tc):
        sa_e, sa_o = half_sums(state * rows(kk_ref, t))
        sa = jnp.where(lo, sa_e, sa_o)
        vk_t = jnp.concatenate([x[:, t * pw:(t + 1) * pw][None] for x in vk], axis=0)
        state = state * rows(w_ref, t) - sa * rows(b_ref, t) + vk_t
        for q in range(g):
            acc[q] = acc[q] + _dot3(jnp.where(step_of_row == t, rm[q], 0.0), state[q], _nt_dot, 1)
    s_ref[...] = state.reshape(s_ref.shape)
    for q, (bi, p) in enumerate(pairs):
        o_ref[bi, :, p * pw:(p + 1) * pw] = jnp.concatenate([acc[q][0:tc], acc[q][tc:2 * tc]], axis=1)

    @pl.when(tb == pl.num_programs(1) - 1)
    def _():
        sout_ref[...] = s_ref[...]


def _rwkv_scan(r, w, k, kk, b, v, s0, batch, seq):
    tc = SCAN_TC
    assert seq % tc == 0 and batch % 2 == 0
    nt = seq // tc
    npair = H_C // 2
    rows3 = lambda x: x.reshape(batch, seq, C_W)
    s0p = s0.reshape(batch, npair, 2, NC, NC).transpose(0, 1, 3, 2, 4).reshape(batch, npair, NC, 2 * NC)
    row_spec = pl.BlockSpec((2, tc, C_W), lambda bb, tb: (bb, tb, 0))
    st_spec = pl.BlockSpec((2, npair, NC, 2 * NC), lambda bb, tb: (bb, 0, 0, 0))
    o, s_fin = pl.pallas_call(
        functools.partial(_rwkv_scan_kernel, tc=tc),
        grid=(batch // 2, nt),
        in_specs=[row_spec] * 6 + [st_spec],
        out_specs=[row_spec, st_spec],
        out_shape=[jax.ShapeDtypeStruct((batch, seq, C_W), F32),
                   jax.ShapeDtypeStruct((batch, npair, NC, 2 * NC), F32)],
        scratch_shapes=[pltpu.VMEM((2, npair, NC, 2 * NC), F32)],
        compiler_params=_cparams(("parallel", "arbitrary")),
        name="rwkv_scan",
    )(rows3(r), rows3(w), rows3(k), rows3(kk), rows3(b), rows3(v), s0p)
    s_new = s_fin.reshape(batch, npair, NC, 2, NC).transpose(0, 1, 3, 2, 4).reshape(batch, H_C, NC, NC)
    return o.reshape(batch * seq, C_W), s_new


def _rwkv_post_kernel(o_ref, bonus_ref, g_ref, lng_ref, lnb_ref, ones_ref, out_ref):
    def seg_mean(x):
        return jnp.dot(x, ones_ref[...], precision=HIGHEST, preferred_element_type=F32) * (1.0 / NC)

    o = o_ref[...]
    oc = o - seg_mean(o)
    on = oc * lax.rsqrt(seg_mean(oc * oc) + GN_EPS_C)
    out_ref[...] = (on * lng_ref[...] + lnb_ref[...] + bonus_ref[...]) * g_ref[...]


def _rwkv_post(o, bonus, g, ln_g, ln_b, ones_bd):
    m = o.shape[0]
    tm = _pick_tile(m, 640)
    tok = pl.BlockSpec((tm, C_W), lambda i: (i, 0))
    vec = pl.BlockSpec((1, C_W), lambda i: (0, 0))
    return pl.pallas_call(
        _rwkv_post_kernel,
        grid=(m // tm,),
        in_specs=[tok, tok, tok, vec, vec, pl.BlockSpec((C_W, C_W), lambda i: (0, 0))],
        out_specs=tok,
        out_shape=jax.ShapeDtypeStruct((m, C_W), F32),
        compiler_params=_cparams(("parallel",)),
        name="rwkv_post",
    )(o, bonus, g, ln_g.reshape(1, C_W), ln_b.reshape(1, C_W), ones_bd)


def _perm_odd_columns():
    offs = np.cumsum([0, C_W, W_LORA, C_W, C_W, A_LORA, G_LORA])
    seg = lambda i: np.arange(offs[i], offs[i + 1])
    pc_perm = np.concatenate([seg(0), seg(2), seg(3), seg(1), seg(4), seg(5)])
    return pc_perm


def kernel(x_prompt, x_sample, cache_a_k, cache_a_v, state_b, state_c_wkv, state_c_shift, cache_d_k, cache_d_v,
           cache_d_logf, page_table, rel_bias, w_in_even, lam_params, subln_g, w_out_even, w_in_odd, c_mu, c_w0,
           c_w2, c_a0, c_a2, c_g2, c_kk, c_ka, c_rk, c_ln_g, c_ln_b, d_bf, w_out_odd, w_router, b_router, w_gate,
           w_up, w_down, ln1_g, ln1_b, ln2_g, ln2_b):
    batch, seq, d = x_prompt.shape
    dbatch, tdec, _ = x_sample.shape
    n_pages = page_table.shape[1]
    n_pool = cache_a_k.shape[1]
    past = n_pages * PAGE
    bt = batch * seq
    ds = dbatch * tdec
    pt_flat = page_table.reshape(-1).astype(jnp.int32)
    gp = _pick_tile(n_pages, PAGES_PER_STEP, 1)

    x = jnp.concatenate([x_prompt.reshape(bt, d), x_sample.reshape(ds, d)], axis=0)
    outs = {}

    def moe_block(x_in, layer):
        gates = _router(x_in, w_router, b_router)
        wgu = jnp.concatenate([w_gate[layer], w_up[layer]], axis=-1).astype(BF16)
        return _moe_ln(x_in, gates, wgu, w_down[layer].astype(BF16), ln2_g[layer], ln2_b[layer])

    lam_init0 = 0.8 - 0.6 * math.exp(-0.3 * 0)
    h0 = _proj(x, w_in_even[0].astype(BF16))
    ka_off = H_A * DKA
    va_off = 2 * H_A * DKA
    ka_all = h0[:, ka_off:ka_off + H_A * DKA]
    va_all = h0[:, va_off:va_off + H_A * DVA]
    outs["a_k_p"] = ka_all[:bt].reshape(1, batch, seq, H_A, DKA)
    outs["a_v_p"] = va_all[:bt].reshape(1, batch, seq, H_A, DVA)
    outs["a_k_s"] = ka_all[bt:].reshape(1, dbatch, tdec, H_A, DKA)
    outs["a_v_s"] = va_all[bt:].reshape(1, dbatch, tdec, H_A, DVA)

    oa_p = _diff_prompt(h0, batch, seq, rel_bias, lam_params[0], subln_g[0], lam_init0)
    qa_s = h0[bt:, 0:H_A * DKA].reshape(dbatch, tdec, H_A, DKA) * (DA ** -0.5)
    q_rows_a = _query_rows(qa_s, 2).astype(BF16)
    tab_d, tself_d = _diff_decode_tables(rel_bias, n_pages, gp, tdec)
    oa_s = _paged_attention(
        "diff", q_rows_a, cache_a_k.reshape(n_pool, PAGE * H_A, DKA), cache_a_v.reshape(n_pool, PAGE * H_A, DVA),
        pt_flat, n_pages, ka_all[bt:].reshape(dbatch, tdec, H_A * DKA), va_all[bt:].reshape(dbatch, tdec, H_A * DVA),
        (tab_d, tself_d, lam_params[0], subln_g[0].reshape(1, DVA)), 1.0, lam_init0)
    ob_p, sb_p = _retention(h0, 0, batch, seq, 0, jnp.zeros((batch, H_B * DKB, H_B * DVB), F32))
    ob_s, sb_s = _retention(h0, bt, dbatch, tdec, past, _state_to_bd(state_b[0].astype(F32)))
    outs["b_p"] = _bd_to_state(sb_p)[None]
    outs["b_s"] = _bd_to_state(sb_s)[None]
    mix_l = jnp.concatenate([oa_p, oa_s.reshape(ds, H_A * DVA)], axis=0)
    mix_r = jnp.concatenate([ob_p, ob_s], axis=0)
    x = _outproj_ln(mix_l, mix_r, x, w_out_even[0].astype(BF16), ln1_g[0], ln1_b[0])
    x = moe_block(x, 0)

    pc_perm = _perm_odd_columns()
    w_odd = w_in_odd[0]
    w_odd_p = jnp.concatenate([w_odd[:, pc_perm], w_odd[:, C_PROJ:],
                               jnp.zeros((d, ODD_IN_PAD - w_odd.shape[1]), w_odd.dtype)], axis=1).astype(BF16)
    h1 = _proj(x, w_odd_p)
    q_off = C_PROJ
    k_off = C_PROJ + D_W
    v_off = C_PROJ + 2 * D_W
    kd_all = h1[:, k_off:k_off + D_W]
    vd_all = h1[:, v_off:v_off + D_W]
    outs["d_k_p"] = kd_all[:bt].reshape(1, batch, seq, H_D, DD)
    outs["d_v_p"] = vd_all[:bt].reshape(1, batch, seq, H_D, DD)
    outs["d_k_s"] = kd_all[bt:].reshape(1, dbatch, tdec, H_D, DD)
    outs["d_v_s"] = vd_all[bt:].reshape(1, dbatch, tdec, H_D, DD)

    bf_pad = jnp.zeros((1, LANES), F32).at[0, :H_D].set(d_bf[0].astype(F32))
    lf_p, c_p = _fox_prep(h1, bf_pad, 0, batch, seq)
    lf_s, c_s = _fox_prep(h1, bf_pad, bt, dbatch, tdec)
    outs["d_lf_p"] = lf_p[:, :H_D].reshape(1, batch, seq, H_D)
    outs["d_lf_s"] = lf_s[:, :H_D].reshape(1, dbatch, tdec, H_D)
    od_p = _fox_prompt(h1, c_p, batch, seq)
    cn = c_s[:, :H_D].reshape(dbatch, tdec, H_D)
    cnt = cn.transpose(0, 2, 1)
    rho = cnt.reshape(dbatch, H_D * tdec, 1)
    logf_pool_t = cache_d_logf[0].astype(F32).transpose(0, 2, 1)
    kap = _suffix_sums(logf_pool_t, pt_flat, dbatch, n_pages)
    qd_s = h1[bt:, q_off:q_off + D_W].reshape(dbatch, tdec, H_D, DD)
    q_rows_d = _query_rows(qd_s, 1).astype(BF16)
    od_s = _paged_attention(
        "fox", q_rows_d, cache_d_k.reshape(n_pool, PAGE * H_D, DD), cache_d_v.reshape(n_pool, PAGE * H_D, DD),
        pt_flat, n_pages, kd_all[bt:].reshape(dbatch, tdec, D_W), vd_all[bt:].reshape(dbatch, tdec, D_W),
        (kap, rho, cnt), DD ** -0.5)

    pc_p = h1[:bt, :C_PROJ].reshape(batch, seq, C_PROJ)
    pc_s = h1[bt:, :C_PROJ].reshape(dbatch, tdec, C_PROJ)
    shift_p = jnp.zeros((batch, 1, C_PROJ), F32)
    shift_s = state_c_shift[0][:, pc_perm][:, None, :].astype(F32)
    prev = jnp.concatenate([
        jnp.concatenate([shift_p, pc_p[:, :-1]], axis=1).reshape(bt, C_PROJ),
        jnp.concatenate([shift_s, pc_s[:, :-1]], axis=1).reshape(ds, C_PROJ)], axis=0)
    inv_perm = np.argsort(pc_perm)
    outs["c_shift_p"] = pc_p[:, -1][:, inv_perm][None]
    outs["c_shift_s"] = pc_s[:, -1][:, inv_perm][None]
    w2p = jnp.concatenate([c_w2[0], jnp.zeros((A_LORA, C_W), F32)], axis=0).astype(BF16)
    a2p = jnp.concatenate([jnp.zeros((W_LORA, C_W), F32), c_a2[0]], axis=0).astype(BF16)
    ones_bd = jnp.asarray(np.kron(np.eye(H_C, dtype=np.float32), np.ones((NC, NC), np.float32)))
    r_, w_, k_, v_, kk_, b_, g_, bonus = _rwkv_prep(
        h1, prev, c_mu[0][pc_perm], c_w0[0], c_a0[0], c_kk[0], c_ka[0], c_rk[0].reshape(-1),
        w2p, a2p, c_g2[0].astype(BF16), ones_bd)
    o_p, sc_p = _rwkv_scan(r_[:bt], w_[:bt], k_[:bt], kk_[:bt], b_[:bt], v_[:bt],
                           jnp.zeros((batch, H_C, NC, NC), F32), batch, seq)
    o_s, sc_s = _rwkv_scan(r_[bt:], w_[bt:], k_[bt:], kk_[bt:], b_[bt:], v_[bt:],
                           state_c_wkv[0].astype(F32), dbatch, tdec)
    outs["c_wkv_p"] = sc_p[None]
    outs["c_wkv_s"] = sc_s[None]
    oc = _rwkv_post(jnp.concatenate([o_p, o_s], axis=0), bonus, g_, c_ln_g[0], c_ln_b[0], ones_bd)
    mix_r = jnp.concatenate([od_p, od_s.reshape(ds, D_W)], axis=0)
    x = _outproj_ln(oc, mix_r, x, w_out_odd[0].astype(BF16), ln1_g[1], ln1_b[1])
    x = moe_block(x, 1)

    y_prompt = x[:bt].reshape(batch, seq, d)
    y_sample = x[bt:].reshape(dbatch, tdec, d)
    return (y_prompt, y_sample,
            outs["a_k_p"], outs["a_v_p"], outs["a_k_s"], outs["a_v_s"],
            outs["b_p"], outs["b_s"],
            outs["c_wkv_p"], outs["c_wkv_s"], outs["c_shift_p"], outs["c_shift_s"],
            outs["d_k_p"], outs["d_v_p"], outs["d_lf_p"], outs["d_k_s"], outs["d_v_s"], outs["d_lf_s"])
```

```python
import functools
import math

import numpy as np
import jax
import jax.numpy as jnp
from jax import lax
from jax.experimental import pallas as pl
from jax.experimental.pallas import tpu as pltpu

F32 = jnp.float32
BF16 = jnp.bfloat16
HIGHEST = lax.Precision.HIGHEST

D_MODEL = 1024
DEPTH = 2
PAGE = 128
MIX_W = D_MODEL // 2
DA = 64
DKA = 2 * DA
DVA = 2 * DA
H_A = MIX_W // DVA
DKB = 64
DVB = 128
H_B = MIX_W // DVB
RET_CHUNK = 128
NC = 64
H_C = MIX_W // NC
W_LORA = 64
A_LORA = 64
G_LORA = 128
GN_EPS_C = 64e-5
DD = 128
H_D = MIX_W // DD
N_BUCKETS = 32
MAX_DISTANCE = 128
N_EXPERTS = 16
N_GROUPS = 4
EPG = N_EXPERTS // N_GROUPS
D_EXPERT = D_MODEL // 4
DN_ALPHA = (2 * DEPTH) ** 0.25
LN_EPS = 1e-5
RMS_EPS = 1e-5
C_W = H_C * NC
C_PROJ = C_W + W_LORA + C_W + C_W + A_LORA + G_LORA
D_W = H_D * DD
EVEN_IN = 2 * H_A * DKA + H_A * DVA + 2 * H_B * DKB + 2 * H_B * DVB
ODD_IN_PAD = C_PROJ + 3 * D_W + 128

LANES = 128
VMEM_LIMIT = 56 * 1024 * 1024
NEG = -1e30
ATT_TQ = 256
ATT_FAR_TILES = 4
FOX_TQ = 256
FOX_FAR_TILES = 4
PAGES_PER_STEP = 16
SUFFIX_PAGES_PER_STEP = 32
SCAN_TC = 8


def _cparams(sem):
    return pltpu.CompilerParams(dimension_semantics=sem, vmem_limit_bytes=VMEM_LIMIT)


def _pick_tile(m, cap, mult=8):
    best = None
    for t in range(mult, min(m, cap) + 1, mult):
        if m % t == 0:
            best = t
    assert best is not None, (m, cap, mult)
    return best


def _layer_norm_rows(z, g, b):
    mu = jnp.mean(z, axis=-1, keepdims=True)
    zc = z - mu
    var = jnp.mean(zc * zc, axis=-1, keepdims=True)
    return zc * lax.rsqrt(var + LN_EPS) * g + b


def _sigmoid(x):
    return 1.0 / (1.0 + jnp.exp(-x))


def _softplus(y):
    return jnp.maximum(y, 0.0) + jnp.log1p(jnp.exp(-jnp.abs(y)))


def _proj_kernel(x_ref, w_ref, o_ref):
    o_ref[...] = jnp.dot(x_ref[...].astype(BF16), w_ref[...], preferred_element_type=F32)


def _proj(x, w_bf):
    m, k = x.shape
    n = w_bf.shape[1]
    tm = _pick_tile(m, 320)
    return pl.pallas_call(
        _proj_kernel,
        grid=(m // tm,),
        in_specs=[pl.BlockSpec((tm, k), lambda i: (i, 0)),
                  pl.BlockSpec((k, n), lambda i: (0, 0))],
        out_specs=pl.BlockSpec((tm, n), lambda i: (i, 0)),
        out_shape=jax.ShapeDtypeStruct((m, n), F32),
        compiler_params=_cparams(("parallel",)),
        name="proj",
    )(x, w_bf)


def _outproj_kernel(a_ref, b_ref, x_ref, w_ref, g_ref, be_ref, o_ref):
    half = a_ref.shape[1]
    y = jnp.dot(a_ref[...].astype(BF16), w_ref[0:half, :], preferred_element_type=F32)
    y = y + jnp.dot(b_ref[...].astype(BF16), w_ref[half:2 * half, :], preferred_element_type=F32)
    z = DN_ALPHA * x_ref[...] + y
    o_ref[...] = _layer_norm_rows(z, g_ref[...], be_ref[...])


def _outproj_ln(mix_l, mix_r, x, w_bf, g, b):
    m, d = x.shape
    half = mix_l.shape[1]
    tm = _pick_tile(m, 640)
    return pl.pallas_call(
        _outproj_kernel,
        grid=(m // tm,),
        in_specs=[pl.BlockSpec((tm, half), lambda i: (i, 0)),
                  pl.BlockSpec((tm, half), lambda i: (i, 0)),
                  pl.BlockSpec((tm, d), lambda i: (i, 0)),
                  pl.BlockSpec((2 * half, d), lambda i: (0, 0)),
                  pl.BlockSpec((1, d), lambda i: (0, 0)),
                  pl.BlockSpec((1, d), lambda i: (0, 0))],
        out_specs=pl.BlockSpec((tm, d), lambda i: (i, 0)),
        out_shape=jax.ShapeDtypeStruct((m, d), F32),
        compiler_params=_cparams(("parallel",)),
        name="outproj_ln",
    )(mix_l, mix_r, x, w_bf, g.reshape(1, d), b.reshape(1, d))


def _router_kernel(x_ref, wr_ref, br_ref, g_ref):
    logits = _nt_dot(wr_ref[...], x_ref[...].astype(BF16))
    mx = jnp.max(logits, axis=0, keepdims=True)
    ex = jnp.exp(logits - mx)
    scores = ex / jnp.sum(ex, axis=0, keepdims=True)
    sel = scores + br_ref[...]
    rows = [sel[e:e + 1, :] for e in range(N_EXPERTS)]
    srow = [scores[e:e + 1, :] for e in range(N_EXPERTS)]
    grp = []
    for gi in range(N_GROUPS):
        a, b, c, d = rows[EPG * gi:EPG * gi + EPG]
        hi1, lo1 = jnp.maximum(a, b), jnp.minimum(a, b)
        hi2, lo2 = jnp.maximum(c, d), jnp.minimum(c, d)
        top1 = jnp.maximum(hi1, hi2)
        top2 = jnp.maximum(jnp.minimum(hi1, hi2), jnp.maximum(lo1, lo2))
        grp.append(top1 + top2)
    best_v = grp[0]
    best_i = jnp.zeros_like(best_v, dtype=jnp.int32)
    for gi in range(1, N_GROUPS):
        better = grp[gi] > best_v
        best_v = jnp.where(better, grp[gi], best_v)
        best_i = jnp.where(better, gi, best_i)
    chosen = []
    for e in range(N_EXPERTS):
        gi = e // EPG
        rank = jnp.zeros_like(best_i)
        for j in range(EPG * gi, EPG * gi + EPG):
            if j == e:
                continue
            beats = (rows[j] >= rows[e]) if j < e else (rows[j] > rows[e])
            rank = rank + beats.astype(jnp.int32)
        chosen.append((best_i == gi) & (rank < 2))
    wsum = jnp.zeros_like(best_v)
    for e in range(N_EXPERTS):
        wsum = wsum + jnp.where(chosen[e], srow[e], 0.0)
    for e in range(N_EXPERTS):
        g_ref[e:e + 1, :] = jnp.where(chosen[e], srow[e] / wsum, 0.0)


def _router(x, w_router, b_router):
    m, d = x.shape
    tm = _pick_tile(m, 1280, LANES)
    gt = pl.pallas_call(
        _router_kernel,
        grid=(m // tm,),
        in_specs=[pl.BlockSpec((tm, d), lambda i: (i, 0)),
                  pl.BlockSpec((N_EXPERTS, d), lambda i: (0, 0)),
                  pl.BlockSpec((N_EXPERTS, 1), lambda i: (0, 0))],
        out_specs=pl.BlockSpec((N_EXPERTS, tm), lambda i: (0, i)),
        out_shape=jax.ShapeDtypeStruct((N_EXPERTS, m), F32),
        compiler_params=_cparams(("parallel",)),
        name="router",
    )(x, w_router.T.astype(BF16), b_router.reshape(N_EXPERTS, 1))
    return gt.T


def _moe_kernel(x_ref, gate_ref, wgu_ref, wd_ref, g_ref, b_ref, o_ref, xb_ref, acc_ref):
    e = pl.program_id(1)

    @pl.when(e == 0)
    def _():
        xb_ref[...] = x_ref[...].astype(BF16)
        acc_ref[...] = jnp.zeros_like(acc_ref)

    h = jnp.dot(xb_ref[...], wgu_ref[0], preferred_element_type=F32)
    gates = gate_ref[...]
    lane = lax.broadcasted_iota(jnp.int32, gates.shape, 1)
    gcol = jnp.sum(jnp.where(lane == e, gates, 0.0), axis=1, keepdims=True)
    hg = h[:, :D_EXPERT]
    hu = h[:, D_EXPERT:]
    act = hg * _sigmoid(hg) * hu * gcol
    acc_ref[...] += jnp.dot(act.astype(BF16), wd_ref[0], preferred_element_type=F32)

    @pl.when(e == N_EXPERTS - 1)
    def _():
        z = DN_ALPHA * x_ref[...] + acc_ref[...]
        o_ref[...] = _layer_norm_rows(z, g_ref[...], b_ref[...])


def _moe_ln(x, gates, wgu_bf, wd_bf, g, b):
    m, d = x.shape
    tm = _pick_tile(m, 640)
    return pl.pallas_call(
        _moe_kernel,
        grid=(m // tm, N_EXPERTS),
        in_specs=[pl.BlockSpec((tm, d), lambda i, e: (i, 0)),
                  pl.BlockSpec((tm, N_EXPERTS), lambda i, e: (i, 0)),
                  pl.BlockSpec((1, d, 2 * D_EXPERT), lambda i, e: (e, 0, 0)),
                  pl.BlockSpec((1, D_EXPERT, d), lambda i, e: (e, 0, 0)),
                  pl.BlockSpec((1, d), lambda i, e: (0, 0)),
                  pl.BlockSpec((1, d), lambda i, e: (0, 0))],
        out_specs=pl.BlockSpec((tm, d), lambda i, e: (i, 0)),
        out_shape=jax.ShapeDtypeStruct((m, d), F32),
        scratch_shapes=[pltpu.VMEM((tm, d), BF16), pltpu.VMEM((tm, d), F32)],
        compiler_params=_cparams(("parallel", "arbitrary")),
        name="moe_ln",
    )(x, gates, wgu_bf, wd_bf, g.reshape(1, d), b.reshape(1, d))


def _softmax_update(s, v_bf, m_ref, l_ref, acc_ref, idx):
    m_prev = m_ref[idx][:, 0:1]
    l_prev = l_ref[idx][:, 0:1]
    m_new = jnp.maximum(m_prev, jnp.max(s, axis=1, keepdims=True))
    alpha = jnp.exp(m_prev - m_new)
    p = jnp.exp(s - m_new)
    l_new = alpha * l_prev + jnp.sum(p, axis=1, keepdims=True)
    acc_ref[idx] = alpha * acc_ref[idx] + jnp.dot(p.astype(BF16), v_bf, preferred_element_type=F32)
    m_ref[idx] = jnp.broadcast_to(m_new, m_ref.shape[1:])
    l_ref[idx] = jnp.broadcast_to(l_new, l_ref.shape[1:])


def _softmax_init(m_ref, l_ref, acc_ref):
    m_ref[...] = jnp.full(m_ref.shape, NEG, F32)
    l_ref[...] = jnp.zeros(l_ref.shape, F32)
    acc_ref[...] = jnp.zeros(acc_ref.shape, F32)


def _far_loops(nfar, body, far_tiles):
    nwide = nfar // far_tiles
    npair = nwide // 2

    def wide_pair(j, carry):
        body(2 * j, far_tiles)
        body(2 * j + 1, far_tiles)
        return carry

    def wide(j, carry):
        body(j, far_tiles)
        return carry

    def single(j, carry):
        body(j, 1)
        return carry

    lax.fori_loop(0, npair, wide_pair, 0)
    lax.fori_loop(2 * npair, nwide, wide, 0)
    lax.fori_loop(nwide * far_tiles, nfar, single, 0)


def _nt_dot(a, b):
    return lax.dot_general(a, b, (((1,), (1,)), ((), ())), preferred_element_type=F32)


def _tn_dot(a, b):
    return lax.dot_general(a, b, (((0,), (0,)), ((), ())), preferred_element_type=F32)


def _t5_bucket_np(dist):
    max_exact = N_BUCKETS // 2
    d = np.maximum(dist, 0)
    ratio = (np.maximum(d, 1).astype(np.float32) / np.float32(max_exact)).astype(np.float32)
    log_ratio = (np.log(ratio).astype(np.float32) / np.float32(math.log(MAX_DISTANCE / max_exact))).astype(np.float32)
    large = np.minimum(max_exact + (log_ratio * np.float32(N_BUCKETS - max_exact)).astype(np.int32), N_BUCKETS - 1)
    return np.where(d < max_exact, d, large).astype(np.int32)


_T5_FAR = int(np.min(np.nonzero(_t5_bucket_np(np.arange(4 * MAX_DISTANCE)) == N_BUCKETS - 1)[0]))
assert np.all(_t5_bucket_np(np.arange(_T5_FAR, 8 * MAX_DISTANCE)) == N_BUCKETS - 1)


def _lam_from_params(lp):
    s01 = jnp.sum(lp[0:1, :] * lp[1:2, :], axis=1, keepdims=True)
    s23 = jnp.sum(lp[2:3, :] * lp[3:4, :], axis=1, keepdims=True)
    return jnp.exp(s01) - jnp.exp(s23)


def _diff_finalize(o0, o1, lam, gain, lam_init):
    o = o0 - lam * o1
    ms = jnp.mean(o * o, axis=-1, keepdims=True)
    return o * lax.rsqrt(ms + RMS_EPS) * gain * (1.0 - lam_init)


def _diff_prompt_kernel(q_ref, k_ref, v_ref, tab_ref, lp_ref, gain_ref, o_ref,
                        kb_ref, vb_ref, m_ref, l_ref, acc_ref, *, tq, lam_init):
    i = pl.program_id(2)

    @pl.when(i == 0)
    def _():
        kb_ref[...] = k_ref[...].astype(BF16)
        vb_ref[...] = v_ref[...].astype(BF16)

    _softmax_init(m_ref, l_ref, acc_ref)
    qf = q_ref[...] * (DA ** -0.5)
    lane = lax.broadcasted_iota(jnp.int32, qf.shape, 1)
    qm = (jnp.where(lane < DA, qf, 0.0).astype(BF16), jnp.where(lane >= DA, qf, 0.0).astype(BF16))

    def chunk_at(ks, width, bias):
        kc = kb_ref[pl.ds(ks, width), :]
        vc = vb_ref[pl.ds(ks, width), :]
        for mi in range(2):
            s = _nt_dot(qm[mi], kc)
            if bias is not None:
                s = s + bias
            _softmax_update(s, vc, m_ref, l_ref, acc_ref, mi)

    def far_chunk(j, tiles):
        width = tiles * tq
        chunk_at(pl.multiple_of(j * width, width), width, None)

    _far_loops(jnp.maximum(i - 1, 0), far_chunk, ATT_FAR_TILES)

    @pl.when(i >= 1)
    def _():
        chunk_at(pl.multiple_of((i - 1) * tq, tq), 2 * tq, jnp.concatenate([tab_ref[0, 1], tab_ref[0, 0]], axis=1))

    @pl.when(i == 0)
    def _():
        chunk_at(0, tq, tab_ref[0, 0])
    lam = _lam_from_params(lp_ref[...]) + lam_init
    o0 = acc_ref[0] / l_ref[0][:, 0:1]
    o1 = acc_ref[1] / l_ref[1][:, 0:1]
    o_ref[...] = _diff_finalize(o0, o1, lam, gain_ref[...], lam_init)


def _diff_tables(rel_bias, tq):
    qi = np.arange(tq)[:, None]
    kj = np.arange(tq)[None, :]
    rb = rel_bias.astype(F32)
    rel = rb - rb[N_BUCKETS - 1][None, :]
    def lookup(bucket):
        onehot = jnp.asarray(bucket)[..., None] == jnp.arange(N_BUCKETS)
        return jnp.sum(jnp.where(onehot[..., None], rel[None, None], 0.0), axis=2)

    diag = jnp.where(jnp.asarray(qi >= kj)[..., None], lookup(_t5_bucket_np(qi - kj)), NEG)
    prev = lookup(_t5_bucket_np(tq + qi - kj))
    return jnp.moveaxis(jnp.stack([diag, prev], 0), -1, 0)


def _diff_prompt(h0, batch, seq, rel_bias, lam_params, subln_g, lam_init):
    tq = min(ATT_TQ, seq)
    assert seq % tq == 0 and tq + 1 >= _T5_FAR
    nq = seq // tq
    tab = _diff_tables(rel_bias, tq)
    kern = functools.partial(_diff_prompt_kernel, tq=tq, lam_init=lam_init)
    qoff, koff, voff = 0, H_A, 2 * H_A
    return pl.pallas_call(
        kern,
        grid=(batch, H_A, nq),
        in_specs=[pl.BlockSpec((tq, DKA), lambda b, h, i: (b * nq + i, qoff + h)),
                  pl.BlockSpec((seq, DKA), lambda b, h, i: (b, koff + h)),
                  pl.BlockSpec((seq, DVA), lambda b, h, i: (b, voff + h)),
                  pl.BlockSpec((1, 2, tq, tq), lambda b, h, i: (h, 0, 0, 0)),
                  pl.BlockSpec((4, DA), lambda b, h, i: (0, 0)),
                  pl.BlockSpec((1, DVA), lambda b, h, i: (0, 0))],
        out_specs=pl.BlockSpec((tq, DVA), lambda b, h, i: (b * nq + i, h)),
        out_shape=jax.ShapeDtypeStruct((batch * seq, H_A * DVA), F32),
        scratch_shapes=[pltpu.VMEM((seq, DKA), BF16), pltpu.VMEM((seq, DVA), BF16),
                        pltpu.VMEM((2, tq, LANES), F32), pltpu.VMEM((2, tq, LANES), F32),
                        pltpu.VMEM((2, tq, DVA), F32)],
        compiler_params=_cparams(("parallel", "parallel", "arbitrary")),
        name="diff_prompt",
    )(h0, h0, h0, tab, lam_params, subln_g.reshape(1, DVA))


def _fox_prompt_kernel(q_ref, k_ref, v_ref, c_ref, kap_ref, o_ref,
                       kb_ref, vb_ref, m_ref, l_ref, acc_ref, *, tq):
    h = pl.program_id(1)
    i = pl.program_id(2)

    @pl.when(i == 0)
    def _():
        kb_ref[...] = k_ref[...].astype(BF16)
        vb_ref[...] = v_ref[...].astype(BF16)

    _softmax_init(m_ref, l_ref, acc_ref)
    qb = q_ref[...].astype(BF16)
    cblk = c_ref[...]
    lane = lax.broadcasted_iota(jnp.int32, cblk.shape, 1)
    rho = jnp.sum(jnp.where(lane == h, cblk, 0.0), axis=1, keepdims=True)

    def chunk(j, masked, tiles=1):
        width = tiles * tq
        ks = pl.multiple_of(j * width, width)
        kc = kb_ref[pl.ds(ks, width), :]
        vc = vb_ref[pl.ds(ks, width), :]
        s = _nt_dot(qb, kc) * (DD ** -0.5) + (rho - kap_ref[0, :, pl.ds(ks, width)])
        if masked:
            r = lax.broadcasted_iota(jnp.int32, s.shape, 0)
            c = lax.broadcasted_iota(jnp.int32, s.shape, 1)
            s = jnp.where(r >= c, s, NEG)
        _softmax_update(s, vc, m_ref, l_ref, acc_ref, 0)

    _far_loops(i, lambda j, tiles: chunk(j, False, tiles), FOX_FAR_TILES)
    chunk(i, True)
    o_ref[...] = acc_ref[0] / l_ref[0][:, 0:1]


def _fox_prompt(h1, c_all, batch, seq):
    tq = min(FOX_TQ, seq)
    assert seq % tq == 0
    nq = seq // tq
    qoff = C_PROJ // LANES
    koff = qoff + H_D
    voff = koff + H_D
    kap = c_all[:batch * seq, :H_D].reshape(batch, seq, H_D).transpose(0, 2, 1).reshape(batch * H_D, 1, seq)
    kern = functools.partial(_fox_prompt_kernel, tq=tq)
    return pl.pallas_call(
        kern,
        grid=(batch, H_D, nq),
        in_specs=[pl.BlockSpec((tq, DD), lambda b, h, i: (b * nq + i, qoff + h)),
                  pl.BlockSpec((seq, DD), lambda b, h, i: (b, koff + h)),
                  pl.BlockSpec((seq, DD), lambda b, h, i: (b, voff + h)),
                  pl.BlockSpec((tq, LANES), lambda b, h, i: (b * nq + i, 0)),
                  pl.BlockSpec((1, 1, seq), lambda b, h, i: (b * H_D + h, 0, 0))],
        out_specs=pl.BlockSpec((tq, DD), lambda b, h, i: (b * nq + i, h)),
        out_shape=jax.ShapeDtypeStruct((batch * seq, D_W), F32),
        scratch_shapes=[pltpu.VMEM((seq, DD), BF16), pltpu.VMEM((seq, DD), BF16),
                        pltpu.VMEM((1, tq, LANES), F32), pltpu.VMEM((1, tq, LANES), F32),
                        pltpu.VMEM((1, tq, DD), F32)],
        compiler_params=_cparams(("parallel", "parallel", "arbitrary")),
        name="fox_prompt",
    )(h1, h1, h1, c_all, kap)


def _fox_prep_kernel(fd_ref, bf_ref, tri_ref, lf_ref, c_ref, carry_ref):
    @pl.when(pl.program_id(1) == 0)
    def _():
        carry_ref[...] = jnp.zeros_like(carry_ref)

    x = fd_ref[...] + bf_ref[...]
    lf = -_softplus(-x)
    inc = jnp.dot(tri_ref[...], lf, precision=HIGHEST, preferred_element_type=F32) + carry_ref[0:1, :]
    lf_ref[...] = lf
    c_ref[...] = inc
    tm = inc.shape[0]
    carry_ref[...] = jnp.broadcast_to(inc[tm - 1:tm, :], carry_ref.shape)


def _fox_prep(h1, bf_pad, row0, batch, seq):
    tm = _pick_tile(seq, 256)
    nt = seq // tm
    assert row0 % tm == 0
    blk0 = row0 // tm
    col = (ODD_IN_PAD - LANES) // LANES
    tri = jnp.asarray(np.tril(np.ones((tm, tm), np.float32)))
    return pl.pallas_call(
        _fox_prep_kernel,
        grid=(batch, nt),
        in_specs=[pl.BlockSpec((tm, LANES), lambda b, i: (blk0 + b * nt + i, col)),
                  pl.BlockSpec((1, LANES), lambda b, i: (0, 0)),
                  pl.BlockSpec((tm, tm), lambda b, i: (0, 0))],
        out_specs=[pl.BlockSpec((tm, LANES), lambda b, i: (b * nt + i, 0)),
                   pl.BlockSpec((tm, LANES), lambda b, i: (b * nt + i, 0))],
        out_shape=[jax.ShapeDtypeStruct((batch * seq, LANES), F32),
                   jax.ShapeDtypeStruct((batch * seq, LANES), F32)],
        scratch_shapes=[pltpu.VMEM((8, LANES), F32)],
        compiler_params=_cparams(("parallel", "arbitrary")),
        name="fox_prep",
    )(h1, bf_pad, tri)


def _suffix_kernel(pt_ref, *refs, gpages):
    lf_refs = refs[:gpages]
    ut_ref = refs[gpages]
    o_ref = refs[gpages + 1]
    carry_ref = refs[gpages + 2]

    @pl.when(pl.program_id(1) == 0)
    def _():
        carry_ref[...] = jnp.zeros_like(carry_ref)

    run = carry_ref[:, 0:1]
    for j in reversed(range(gpages)):
        lf = lf_refs[j][...]
        incl = jnp.dot(lf, ut_ref[...], precision=HIGHEST, preferred_element_type=F32)
        o_ref[0, :, j * PAGE:(j + 1) * PAGE] = incl - lf + run
        run = run + incl[:, 0:1]
    carry_ref[...] = jnp.broadcast_to(run, carry_ref.shape)


def _suffix_sums(logf_pool_t, page_table_flat, dbatch, n_pages):
    gp = _pick_tile(n_pages, SUFFIX_PAGES_PER_STEP, 1)
    ng = n_pages // gp
    ut = jnp.asarray(np.tril(np.ones((PAGE, PAGE), np.float32)))

    def page_map(j):
        return lambda b, g, pt: (pt[b * n_pages + (ng - 1 - g) * gp + j], 0, 0)

    in_specs = [pl.BlockSpec((None, H_D, PAGE), page_map(j)) for j in range(gp)]
    in_specs.append(pl.BlockSpec((PAGE, PAGE), lambda b, g, pt: (0, 0)))
    return pl.pallas_call(
        functools.partial(_suffix_kernel, gpages=gp),
        grid_spec=pltpu.PrefetchScalarGridSpec(
            num_scalar_prefetch=1,
            grid=(dbatch, ng),
            in_specs=in_specs,
            out_specs=pl.BlockSpec((1, H_D, gp * PAGE), lambda b, g, pt: (b, 0, ng - 1 - g)),
            scratch_shapes=[pltpu.VMEM((H_D, LANES), F32)]),
        out_shape=jax.ShapeDtypeStruct((dbatch, H_D, n_pages * PAGE), F32),
        compiler_params=_cparams(("parallel", "arbitrary")),
        name="fox_suffix",
    )(page_table_flat, *([logf_pool_t] * gp), ut)


def _paged_kernel(pt_ref, *refs, gpages, tdec, mode, scale, lam_init, chunk, nheads):
    q_ref = refs[0]
    k_refs = refs[1:1 + gpages]
    v_refs = refs[1 + gpages:1 + 2 * gpages]
    rest = refs[1 + 2 * gpages:]
    if mode == "diff":
        tab_ref, tself_ref, kn_ref, vn_ref, lp_ref, gain_ref, o_ref, s_ref, vb_ref = rest
    else:
        kap_ref, rho_ref, cnt_ref, kn_ref, vn_ref, o_ref, s_ref, vb_ref = rest
    g = pl.program_id(1)
    ng = pl.num_programs(1)
    gw = gpages * PAGE
    rows, past = s_ref.shape
    rph = rows // nheads
    dh = q_ref.shape[2]
    off = pl.multiple_of(g * gw, gw)

    def head_rows(ref, h):
        return ref[pl.ds(h, PAGE, stride=nheads), :]

    for h in range(nheads):
        hs = slice(h * rph, (h + 1) * rph)
        kh = jnp.concatenate([head_rows(r, h).astype(BF16) for r in k_refs], axis=0)
        s = _nt_dot(q_ref[0, hs, :], kh)
        if scale != 1.0:
            s = s * scale
        if mode == "diff":
            s = s + tab_ref[0, hs, :]
        else:
            s = (s + rho_ref[0, hs, :]) + kap_ref[0, h:h + 1, :]
        s_ref[hs, pl.ds(off, gw)] = s
        for j in range(gpages):
            vb_ref[h, pl.ds(off + j * PAGE, PAGE), :] = head_rows(v_refs[j], h).astype(BF16)

    @pl.when(g == ng - 1)
    def _():
        tails = []
        for h in range(nheads):
            hs = slice(h * rph, (h + 1) * rph)
            kn = kn_ref[0, :, h * dh:(h + 1) * dh].astype(BF16)
            t = _nt_dot(q_ref[0, hs, :], kn)
            if scale != 1.0:
                t = t * scale
            if mode == "diff":
                t = t + tself_ref[hs, :]
            else:
                t = (t + rho_ref[0, hs, :]) - cnt_ref[0, h:h + 1, :]
                r = lax.broadcasted_iota(jnp.int32, t.shape, 0)
                c = lax.broadcasted_iota(jnp.int32, t.shape, 1)
                t = jnp.where(c <= r, t, NEG)
            tails.append(t)
        tail = jnp.concatenate(tails, axis=0)
        if mode == "diff":
            lam = _lam_from_params(lp_ref[...]) + lam_init
        nchunk = past // chunk

        def sc(ci, hs=slice(None)):
            return s_ref[hs, pl.ds(pl.multiple_of(ci * chunk, chunk), chunk)]

        m = lax.fori_loop(0, nchunk, lambda ci, mm: jnp.maximum(mm, jnp.max(sc(ci), axis=1, keepdims=True)),
                          jnp.max(tail, axis=1, keepdims=True))
        den = lax.fori_loop(0, nchunk, lambda ci, dd: dd + jnp.sum(jnp.exp(sc(ci) - m), axis=1, keepdims=True),
                            jnp.sum(jnp.exp(tail - m), axis=1, keepdims=True))

        def weights(sv, h):
            hs = slice(h * rph, (h + 1) * rph)
            p = jnp.exp(sv - m[hs]) / den[hs]
            if mode == "diff":
                p = p[0:tdec] - lam * p[tdec:2 * tdec]
            return p.astype(BF16)

        def pv_body(ci, accs):
            cs = pl.multiple_of(ci * chunk, chunk)
            return tuple(
                accs[h] + jnp.dot(weights(sc(ci, slice(h * rph, (h + 1) * rph)), h), vb_ref[h, pl.ds(cs, chunk), :],
                                  preferred_element_type=F32)
                for h in range(nheads))

        init = tuple(
            jnp.dot(weights(tails[h], h), vn_ref[0, :, h * dh:(h + 1) * dh].astype(BF16), preferred_element_type=F32)
            for h in range(nheads))
        outs = lax.fori_loop(0, nchunk, pv_body, init)
        for h in range(nheads):
            oh = outs[h]
            if mode == "diff":
                ms = jnp.mean(oh * oh, axis=-1, keepdims=True)
                oh = oh * lax.rsqrt(ms + RMS_EPS) * gain_ref[...] * (1.0 - lam_init)
            o_ref[0, :, h * dh:(h + 1) * dh] = oh


def _query_rows(q, n_maps):
    db, t, nh, w = q.shape
    dm = w // n_maps
    qh = q.transpose(0, 2, 1, 3)
    sel = jnp.asarray(np.kron(np.eye(n_maps, dtype=np.float32), np.ones((1, dm), np.float32)))
    rows = qh[:, :, None, :, :] * sel[None, None, :, None, :]
    return rows.reshape(db, nh * n_maps * t, w)


def _paged_attention(mode, q_rows_bf, k_pool, v_pool, page_table_flat, n_pages, k_new, v_new, extra, scale,
                     lam_init=0.0):
    dbatch, rows, dh = q_rows_bf.shape
    tdec = k_new.shape[1]
    width = k_new.shape[2]
    nheads = width // dh
    gp = _pick_tile(n_pages, PAGES_PER_STEP, 1)
    ng = n_pages // gp
    past = n_pages * PAGE
    chunk = _pick_tile(past, 2048, LANES)
    pad_rows = lambda x: jnp.pad(x, ((0, 0), (0, PAGE - tdec), (0, 0)))
    k_new = pad_rows(k_new)
    v_new = pad_rows(v_new)

    def page_map(j):
        return lambda b, g, pt: (pt[b * n_pages + g * gp + j], 0, 0)

    in_specs = [pl.BlockSpec((1, rows, dh), lambda b, g, pt: (b, 0, 0))]
    in_specs += [pl.BlockSpec((None, PAGE * nheads, dh), page_map(j)) for j in range(gp)]
    in_specs += [pl.BlockSpec((None, PAGE * nheads, dh), page_map(j)) for j in range(gp)]
    new_spec = pl.BlockSpec((1, PAGE, width), lambda b, g, pt: (b, 0, 0))
    if mode == "diff":
        tab, tself, lam_params, gain = extra
        tself = jnp.pad(tself, ((0, 0), (0, PAGE - tdec)), constant_values=NEG)
        in_specs += [pl.BlockSpec((1, rows, gp * PAGE), lambda b, g, pt: ((g + 1) // ng, 0, 0)),
                     pl.BlockSpec((rows, PAGE), lambda b, g, pt: (0, 0)),
                     new_spec, new_spec,
                     pl.BlockSpec((4, DA), lambda b, g, pt: (0, 0)),
                     pl.BlockSpec((1, DVA), lambda b, g, pt: (0, 0))]
        operands = [tab, tself, k_new, v_new, lam_params, gain]
    else:
        kap, rho, cnt = extra
        cnt = jnp.pad(cnt, ((0, 0), (0, 0), (0, PAGE - tdec)))
        in_specs += [pl.BlockSpec((1, H_D, gp * PAGE), lambda b, g, pt: (b, 0, g)),
                     pl.BlockSpec((1, rows, 1), lambda b, g, pt: (b, 0, 0)),
                     pl.BlockSpec((1, H_D, PAGE), lambda b, g, pt: (b, 0, 0)),
                     new_spec, new_spec]
        operands = [kap, rho, cnt, k_new, v_new]
    kern = functools.partial(_paged_kernel, gpages=gp, tdec=tdec, mode=mode, scale=scale, lam_init=lam_init,
                             chunk=chunk, nheads=nheads)
    return pl.pallas_call(
        kern,
        grid_spec=pltpu.PrefetchScalarGridSpec(
            num_scalar_prefetch=1,
            grid=(dbatch, ng),
            in_specs=in_specs,
            out_specs=pl.BlockSpec((1, tdec, width), lambda b, g, pt: (b, 0, 0)),
            scratch_shapes=[pltpu.VMEM((rows, past), F32), pltpu.VMEM((nheads, past, dh), BF16)]),
        out_shape=jax.ShapeDtypeStruct((dbatch, tdec, width), F32),
        compiler_params=_cparams(("parallel", "arbitrary")),
        name="paged_" + mode,
    )(page_table_flat, q_rows_bf, *([k_pool] * gp), *([v_pool] * gp), *operands)


def _diff_decode_tables(rel_bias, n_pages, gp, tdec):
    rb = rel_bias.astype(F32)
    rel = (rb - rb[N_BUCKETS - 1][None, :]).T
    past = n_pages * PAGE
    t = np.arange(tdec)[:, None]
    kpos = past - gp * PAGE + np.arange(gp * PAGE)[None, :]
    assert PAGE + 1 >= _T5_FAR
    last = rel[:, _t5_bucket_np(past + t - kpos)]
    s = np.arange(tdec)[None, :]
    selfb = jnp.where(jnp.asarray(t >= s)[None], rel[:, _t5_bucket_np(t - s)], NEG)
    rep = lambda x: jnp.broadcast_to(x[:, None], (H_A, 2) + x.shape[1:]).reshape((H_A * 2 * tdec,) + x.shape[2:])
    last_r = rep(last)
    tab = jnp.stack([jnp.zeros_like(last_r), last_r], 0)
    return tab, rep(selfb)


def _retention_kernel(q_ref, k_ref, v_ref, gb_ref, cos_ref, sin_ref, qdec_ref, kdec_ref, dmat_ref,
                      gc_ref, bd_ref, s0_ref, o_ref, sout_ref, s_ref):
    c = pl.program_id(1)

    @pl.when(c == 0)
    def _():
        s_ref[...] = s0_ref[0]

    cos = cos_ref[...]
    sin = sin_ref[...]
    width = cos.shape[1]
    lane = lax.broadcasted_iota(jnp.int32, cos.shape, 1)
    first_half = (lane % DKB) < (DKB // 2)

    def rot(x):
        swapped = jnp.where(first_half, -pltpu.roll(x, width - DKB // 2, 1), pltpu.roll(x, DKB // 2, 1))
        return x * cos + swapped * sin

    qr = rot(q_ref[...])
    kr = rot(k_ref[...]) * (DKB ** -0.5)
    v = v_ref[...]
    vb = v.astype(BF16)
    state = s_ref[...]
    cross = jnp.dot((qr * qdec_ref[...]).astype(BF16), state.astype(BF16), preferred_element_type=F32)
    lane_pair = lax.broadcasted_iota(jnp.int32, (qr.shape[0], LANES), 1)
    gate = gb_ref[...]
    for h in range(H_B):
        cb = (h * DKB) // LANES
        lo = (h * DKB) % LANES
        qh = qr[:, cb * LANES:(cb + 1) * LANES]
        qh = jnp.where((lane_pair >= lo) & (lane_pair < lo + DKB), qh, 0.0).astype(BF16)
        kh = kr[:, cb * LANES:(cb + 1) * LANES].astype(BF16)
        att = _nt_dot(qh, kh) * dmat_ref[h]
        oh = jnp.dot(att.astype(BF16), vb[:, h * DVB:(h + 1) * DVB], preferred_element_type=F32)
        oh = oh + cross[:, h * DVB:(h + 1) * DVB]
        ms = jnp.mean(oh * oh, axis=-1, keepdims=True)
        gh = gate[:, h * DVB:(h + 1) * DVB]
        o_ref[:, h * DVB:(h + 1) * DVB] = oh * lax.rsqrt(ms + RMS_EPS) * (gh * _sigmoid(gh))
    upd = _tn_dot((kr * kdec_ref[...]).astype(BF16), vb)
    new_state = state * gc_ref[...] + upd * bd_ref[...]
    s_ref[...] = new_state

    @pl.when(c == pl.num_programs(1) - 1)
    def _():
        sout_ref[0] = new_state


def _retention(h0, row0, batch, seq, start, s0_bd):
    ch = RET_CHUNK if seq % RET_CHUNK == 0 else seq
    n = seq // ch
    assert row0 % ch == 0
    blk0 = row0 // ch
    kw = H_B * DKB
    vw = H_B * DVB
    qcol = (2 * H_A * DKA + H_A * DVA) // kw
    kcol = qcol + 1
    vcol = (2 * H_A * DKA + H_A * DVA + 2 * kw) // vw
    gcol = vcol + 1
    half = DKB // 2
    inv = 1.0 / (10000.0 ** (jnp.arange(half, dtype=F32) / half))
    pos = (start + jnp.arange(seq)).astype(F32)
    ang = pos[:, None] * inv[None, :]
    cos = jnp.tile(jnp.cos(ang), (1, 2 * H_B))
    sin = jnp.tile(jnp.sin(ang), (1, 2 * H_B))
    log_g = jnp.log(1.0 - jnp.exp2(-5.0 - jnp.arange(H_B, dtype=F32)))
    idx = jnp.arange(ch, dtype=F32)
    diff = idx[:, None] - idx[None, :]
    dmat = jnp.where(diff >= 0, jnp.exp(log_g[:, None, None] * jnp.maximum(diff, 0.0)), 0.0)
    qdec = jnp.repeat(jnp.exp(log_g[None, :] * (idx[:, None] + 1.0)), DKB, axis=1)
    kdec = jnp.repeat(jnp.exp(log_g[None, :] * (ch - 1.0 - idx[:, None])), DKB, axis=1)
    gc = jnp.broadcast_to(jnp.repeat(jnp.exp(log_g * ch), DKB)[:, None], (kw, vw))
    bd = jnp.asarray(np.kron(np.eye(H_B, dtype=np.float32), np.ones((DKB, DVB), np.float32)))
    out, s_fin = pl.pallas_call(
        _retention_kernel,
        grid=(batch, n),
        in_specs=[pl.BlockSpec((ch, kw), lambda b, c: (blk0 + b * n + c, qcol)),
                  pl.BlockSpec((ch, kw), lambda b, c: (blk0 + b * n + c, kcol)),
                  pl.BlockSpec((ch, vw), lambda b, c: (blk0 + b * n + c, vcol)),
                  pl.BlockSpec((ch, vw), lambda b, c: (blk0 + b * n + c, gcol)),
                  pl.BlockSpec((ch, kw), lambda b, c: (c, 0)),
                  pl.BlockSpec((ch, kw), lambda b, c: (c, 0)),
                  pl.BlockSpec((ch, kw), lambda b, c: (0, 0)),
                  pl.BlockSpec((ch, kw), lambda b, c: (0, 0)),
                  pl.BlockSpec((H_B, ch, ch), lambda b, c: (0, 0, 0)),
                  pl.BlockSpec((kw, vw), lambda b, c: (0, 0)),
                  pl.BlockSpec((kw, vw), lambda b, c: (0, 0)),
                  pl.BlockSpec((1, kw, vw), lambda b, c: (b, 0, 0))],
        out_specs=[pl.BlockSpec((ch, vw), lambda b, c: (b * n + c, 0)),
                   pl.BlockSpec((1, kw, vw), lambda b, c: (b, 0, 0))],
        out_shape=[jax.ShapeDtypeStruct((batch * seq, vw), F32),
                   jax.ShapeDtypeStruct((batch, kw, vw), F32)],
        scratch_shapes=[pltpu.VMEM((kw, vw), F32)],
        compiler_params=_cparams(("parallel", "arbitrary")),
        name="retention",
    )(h0, h0, h0, h0, cos, sin, qdec, kdec, dmat, gc, bd, s0_bd)
    return out, s_fin


def _state_to_bd(s):
    b = s.shape[0]
    eye = jnp.eye(H_B, dtype=s.dtype)
    return (s[:, :, :, None, :] * eye[None, :, None, :, None]).reshape(b, H_B * DKB, H_B * DVB)


def _bd_to_state(sbd):
    b = sbd.shape[0]
    s5 = sbd.reshape(b, H_B, DKB, H_B, DVB)
    return jnp.stack([s5[:, h, :, h, :] for h in range(H_B)], axis=1)


def _rwkv_prep_kernel(pc_ref, prev_ref, mu_ref, w0_ref, a0_ref, kkw_ref, kaw_ref, rk_ref,
                      w2_ref, a2_ref, g2_ref, ones_ref,
                      r_ref, w_ref, k_ref, v_ref, kk_ref, b_ref, g_ref, bonus_ref):
    pc = pc_ref[...]
    pm = pc + (prev_ref[...] - pc) * mu_ref[...]
    r = pm[:, 0:C_W]
    k = pm[:, C_W:2 * C_W]
    v = pm[:, 2 * C_W:3 * C_W]
    wa = pm[:, 3 * C_W:3 * C_W + W_LORA + A_LORA]
    gd = pm[:, 3 * C_W + W_LORA + A_LORA:]
    wl = jnp.dot(jnp.tanh(wa).astype(BF16), w2_ref[...], preferred_element_type=F32)
    w_log = -_softplus(-(w0_ref[...] + wl)) - 0.5
    decay = jnp.exp(-jnp.exp(w_log))
    a = _sigmoid(a0_ref[...] + jnp.dot(wa.astype(BF16), a2_ref[...], preferred_element_type=F32))
    g = jnp.dot(_sigmoid(gd).astype(BF16), g2_ref[...], preferred_element_type=F32)

    def seg_sum(x):
        return jnp.dot(x, ones_ref[...], precision=HIGHEST, preferred_element_type=F32)

    kkr = k * kkw_ref[...]
    kk = kkr / jnp.maximum(jnp.sqrt(seg_sum(kkr * kkr)), 1e-12)
    k2 = k * (1.0 + (a - 1.0) * kaw_ref[...])
    r_ref[...] = r
    w_ref[...] = decay
    k_ref[...] = k2
    v_ref[...] = v
    kk_ref[...] = kk
    b_ref[...] = kk * a
    g_ref[...] = g
    bonus_ref[...] = seg_sum(r * k2 * rk_ref[...]) * v


def _rwkv_prep(h1, prev, mu_p, w0, a0, kkw, kaw, rk, w2p, a2p, g2, ones_bd):
    m = h1.shape[0]
    tm = _pick_tile(m, 320)
    row = lambda x: x.reshape(1, -1)
    tok = lambda w: pl.BlockSpec((tm, w), lambda i: (i, 0))
    full = lambda a: pl.BlockSpec(a.shape, lambda i: (0, 0))
    small = [row(mu_p), row(w0), row(a0), row(kkw), row(kaw), row(rk), w2p, a2p, g2, ones_bd]
    outs = pl.pallas_call(
        _rwkv_prep_kernel,
        grid=(m // tm,),
        in_specs=[tok(C_PROJ), tok(C_PROJ)] + [full(a) for a in small],
        out_specs=[tok(C_W)] * 8,
        out_shape=[jax.ShapeDtypeStruct((m, C_W), F32)] * 8,
        compiler_params=_cparams(("parallel",)),
        name="rwkv_prep",
    )(h1, prev, *small)
    return outs


def _split_bf16(x):
    hi = x.astype(BF16)
    return hi, (x - hi.astype(F32)).astype(BF16)


def _dot3(a, b, dot, axis):
    ah, al = _split_bf16(a)
    bh, bl = _split_bf16(b)
    return dot(jnp.concatenate([ah, ah, al], axis=axis), jnp.concatenate([bh, bl, bh], axis=axis))


def _rwkv_scan_kernel(r_ref, w_ref, k_ref, kk_ref, b_ref, v_ref, s0_ref, o_ref, sout_ref, s_ref, *, tc):
    tb = pl.program_id(1)

    @pl.when(tb == 0)
    def _():
        s_ref[...] = s0_ref[...]

    nb, npair = s_ref.shape[0], s_ref.shape[1]
    pairs = [(bi, p) for bi in range(nb) for p in range(npair)]
    g = len(pairs)
    pw = 2 * NC
    shape = (g, NC, pw)
    lo = lax.broadcasted_iota(jnp.int32, shape, 2) < NC
    lo_rows = lax.broadcasted_iota(jnp.int32, (tc, pw), 1) < NC
    step_of_row = lax.broadcasted_iota(jnp.int32, (2 * tc, pw), 0) % tc

    def rows(ref, t):
        return jnp.concatenate([ref[bi, t:t + 1, p * pw:(p + 1) * pw][None] for bi, p in pairs], axis=0)

    def half_sums(x):
        even = jnp.sum(jnp.where(lo, x, 0.0), axis=2, keepdims=True)
        odd = jnp.sum(jnp.where(lo, 0.0, x), axis=2, keepdims=True)
        return even, odd

    def parity_rows(x):
        return jnp.concatenate([jnp.where(lo_rows, x, 0.0), jnp.where(lo_rows, 0.0, x)], axis=0)

    vk, rm = [], []
    for bi, p in pairs:
        sl = slice(p * pw, (p + 1) * pw)
        kp = parity_rows(k_ref[bi, :, sl])
        kb = jnp.concatenate([jnp.where(step_of_row == t, kp, 0.0) for t in range(tc)], axis=1)
        v8 = v_ref[bi, :, sl]
        va = jnp.concatenate([v8[:, 0:NC], v8[:, NC:pw]], axis=0)
        vk.append(_dot3(va, kb, _tn_dot, 0))
        rm.append(parity_rows(r_ref[bi, :, sl]))

    state = s_ref[...].reshape(shape)
    acc = [jnp.zeros((2 * tc, NC), F32) for _ in pairs]
    for t in range(tc):
        sa_e, sa_o = half_sums(state * rows(kk_ref, t))
        sa = jnp.where(lo, sa_e, sa_o)
        vk_t = jnp.concatenate([x[:, t * pw:(t + 1) * pw][None] for x in vk], axis=0)
        state = state * rows(w_ref, t) - sa * rows(b_ref, t) + vk_t
        for q in range(g):
            acc[q] = acc[q] + _dot3(jnp.where(step_of_row == t, rm[q], 0.0), state[q], _nt_dot, 1)
    s_ref[...] = state.reshape(s_ref.shape)
    for q, (bi, p) in enumerate(pairs):
        o_ref[bi, :, p * pw:(p + 1) * pw] = jnp.concatenate([acc[q][0:tc], acc[q][tc:2 * tc]], axis=1)

    @pl.when(tb == pl.num_programs(1) - 1)
    def _():
        sout_ref[...] = s_ref[...]


def _rwkv_scan(r, w, k, kk, b, v, s0, batch, seq):
    tc = SCAN_TC
    assert seq % tc == 0 and batch % 2 == 0
    nt = seq // tc
    npair = H_C // 2
    rows3 = lambda x: x.reshape(batch, seq, C_W)
    s0p = s0.reshape(batch, npair, 2, NC, NC).transpose(0, 1, 3, 2, 4).reshape(batch, npair, NC, 2 * NC)
    row_spec = pl.BlockSpec((2, tc, C_W), lambda bb, tb: (bb, tb, 0))
    st_spec = pl.BlockSpec((2, npair, NC, 2 * NC), lambda bb, tb: (bb, 0, 0, 0))
    o, s_fin = pl.pallas_call(
        functools.partial(_rwkv_scan_kernel, tc=tc),
        grid=(batch // 2, nt),
        in_specs=[row_spec] * 6 + [st_spec],
        out_specs=[row_spec, st_spec],
        out_shape=[jax.ShapeDtypeStruct((batch, seq, C_W), F32),
                   jax.ShapeDtypeStruct((batch, npair, NC, 2 * NC), F32)],
        scratch_shapes=[pltpu.VMEM((2, npair, NC, 2 * NC), F32)],
        compiler_params=_cparams(("parallel", "arbitrary")),
        name="rwkv_scan",
    )(rows3(r), rows3(w), rows3(k), rows3(kk), rows3(b), rows3(v), s0p)
    s_new = s_fin.reshape(batch, npair, NC, 2, NC).transpose(0, 1, 3, 2, 4).reshape(batch, H_C, NC, NC)
    return o.reshape(batch * seq, C_W), s_new


def _rwkv_post_kernel(o_ref, bonus_ref, g_ref, lng_ref, lnb_ref, ones_ref, out_ref):
    def seg_mean(x):
        return jnp.dot(x, ones_ref[...], precision=HIGHEST, preferred_element_type=F32) * (1.0 / NC)

    o = o_ref[...]
    oc = o - seg_mean(o)
    on = oc * lax.rsqrt(seg_mean(oc * oc) + GN_EPS_C)
    out_ref[...] = (on * lng_ref[...] + lnb_ref[...] + bonus_ref[...]) * g_ref[...]


def _rwkv_post(o, bonus, g, ln_g, ln_b, ones_bd):
    m = o.shape[0]
    tm = _pick_tile(m, 640)
    tok = pl.BlockSpec((tm, C_W), lambda i: (i, 0))
    vec = pl.BlockSpec((1, C_W), lambda i: (0, 0))
    return pl.pallas_call(
        _rwkv_post_kernel,
        grid=(m // tm,),
        in_specs=[tok, tok, tok, vec, vec, pl.BlockSpec((C_W, C_W), lambda i: (0, 0))],
        out_specs=tok,
        out_shape=jax.ShapeDtypeStruct((m, C_W), F32),
        compiler_params=_cparams(("parallel",)),
        name="rwkv_post",
    )(o, bonus, g, ln_g.reshape(1, C_W), ln_b.reshape(1, C_W), ones_bd)


def _perm_odd_columns():
    offs = np.cumsum([0, C_W, W_LORA, C_W, C_W, A_LORA, G_LORA])
    seg = lambda i: np.arange(offs[i], offs[i + 1])
    pc_perm = np.concatenate([seg(0), seg(2), seg(3), seg(1), seg(4), seg(5)])
    return pc_perm


def kernel(x_prompt, x_sample, cache_a_k, cache_a_v, state_b, state_c_wkv, state_c_shift, cache_d_k, cache_d_v,
           cache_d_logf, page_table, rel_bias, w_in_even, lam_params, subln_g, w_out_even, w_in_odd, c_mu, c_w0,
           c_w2, c_a0, c_a2, c_g2, c_kk, c_ka, c_rk, c_ln_g, c_ln_b, d_bf, w_out_odd, w_router, b_router, w_gate,
           w_up, w_down, ln1_g, ln1_b, ln2_g, ln2_b):
    batch, seq, d = x_prompt.shape
    dbatch, tdec, _ = x_sample.shape
    n_pages = page_table.shape[1]
    n_pool = cache_a_k.shape[1]
    past = n_pages * PAGE
    bt = batch * seq
    ds = dbatch * tdec
    pt_flat = page_table.reshape(-1).astype(jnp.int32)
    gp = _pick_tile(n_pages, PAGES_PER_STEP, 1)

    x = jnp.concatenate([x_prompt.reshape(bt, d), x_sample.reshape(ds, d)], axis=0)
    outs = {}

    def moe_block(x_in, layer):
        gates = _router(x_in, w_router, b_router)
        wgu = jnp.concatenate([w_gate[layer], w_up[layer]], axis=-1).astype(BF16)
        return _moe_ln(x_in, gates, wgu, w_down[layer].astype(BF16), ln2_g[layer], ln2_b[layer])

    lam_init0 = 0.8 - 0.6 * math.exp(-0.3 * 0)
    h0 = _proj(x, w_in_even[0].astype(BF16))
    ka_off = H_A * DKA
    va_off = 2 * H_A * DKA
    ka_all = h0[:, ka_off:ka_off + H_A * DKA]
    va_all = h0[:, va_off:va_off + H_A * DVA]
    outs["a_k_p"] = ka_all[:bt].reshape(1, batch, seq, H_A, DKA)
    outs["a_v_p"] = va_all[:bt].reshape(1, batch, seq, H_A, DVA)
    outs["a_k_s"] = ka_all[bt:].reshape(1, dbatch, tdec, H_A, DKA)
    outs["a_v_s"] = va_all[bt:].reshape(1, dbatch, tdec, H_A, DVA)

    oa_p = _diff_prompt(h0, batch, seq, rel_bias, lam_params[0], subln_g[0], lam_init0)
    qa_s = h0[bt:, 0:H_A * DKA].reshape(dbatch, tdec, H_A, DKA) * (DA ** -0.5)
    q_rows_a = _query_rows(qa_s, 2).astype(BF16)
    tab_d, tself_d = _diff_decode_tables(rel_bias, n_pages, gp, tdec)
    oa_s = _paged_attention(
        "diff", q_rows_a, cache_a_k.reshape(n_pool, PAGE * H_A, DKA), cache_a_v.reshape(n_pool, PAGE * H_A, DVA),
        pt_flat, n_pages, ka_all[bt:].reshape(dbatch, tdec, H_A * DKA), va_all[bt:].reshape(dbatch, tdec, H_A * DVA),
        (tab_d, tself_d, lam_params[0], subln_g[0].reshape(1, DVA)), 1.0, lam_init0)
    ob_p, sb_p = _retention(h0, 0, batch, seq, 0, jnp.zeros((batch, H_B * DKB, H_B * DVB), F32))
    ob_s, sb_s = _retention(h0, bt, dbatch, tdec, past, _state_to_bd(state_b[0].astype(F32)))
    outs["b_p"] = _bd_to_state(sb_p)[None]
    outs["b_s"] = _bd_to_state(sb_s)[None]
    mix_l = jnp.concatenate([oa_p, oa_s.reshape(ds, H_A * DVA)], axis=0)
    mix_r = jnp.concatenate([ob_p, ob_s], axis=0)
    x = _outproj_ln(mix_l, mix_r, x, w_out_even[0].astype(BF16), ln1_g[0], ln1_b[0])
    x = moe_block(x, 0)

    pc_perm = _perm_odd_columns()
    w_odd = w_in_odd[0]
    w_odd_p = jnp.concatenate([w_odd[:, pc_perm], w_odd[:, C_PROJ:],
                               jnp.zeros((d, ODD_IN_PAD - w_odd.shape[1]), w_odd.dtype)], axis=1).astype(BF16)
    h1 = _proj(x, w_odd_p)
    q_off = C_PROJ
    k_off = C_PROJ + D_W
    v_off = C_PROJ + 2 * D_W
    kd_all = h1[:, k_off:k_off + D_W]
    vd_all = h1[:, v_off:v_off + D_W]
    outs["d_k_p"] = kd_all[:bt].reshape(1, batch, seq, H_D, DD)
    outs["d_v_p"] = vd_all[:bt].reshape(1, batch, seq, H_D, DD)
    outs["d_k_s"] = kd_all[bt:].reshape(1, dbatch, tdec, H_D, DD)
    outs["d_v_s"] = vd_all[bt:].reshape(1, dbatch, tdec, H_D, DD)

    bf_pad = jnp.zeros((1, LANES), F32).at[0, :H_D].set(d_bf[0].astype(F32))
    lf_p, c_p = _fox_prep(h1, bf_pad, 0, batch, seq)
    lf_s, c_s = _fox_prep(h1, bf_pad, bt, dbatch, tdec)
    outs["d_lf_p"] = lf_p[:, :H_D].reshape(1, batch, seq, H_D)
    outs["d_lf_s"] = lf_s[:, :H_D].reshape(1, dbatch, tdec, H_D)
    od_p = _fox_prompt(h1, c_p, batch, seq)
    cn = c_s[:, :H_D].reshape(dbatch, tdec, H_D)
    cnt = cn.transpose(0, 2, 1)
    rho = cnt.reshape(dbatch, H_D * tdec, 1)
    logf_pool_t = cache_d_logf[0].astype(F32).transpose(0, 2, 1)
    kap = _suffix_sums(logf_pool_t, pt_flat, dbatch, n_pages)
    qd_s = h1[bt:, q_off:q_off + D_W].reshape(dbatch, tdec, H_D, DD)
    q_rows_d = _query_rows(qd_s, 1).astype(BF16)
    od_s = _paged_attention(
        "fox", q_rows_d, cache_d_k.reshape(n_pool, PAGE * H_D, DD), cache_d_v.reshape(n_pool, PAGE * H_D, DD),
        pt_flat, n_pages, kd_all[bt:].reshape(dbatch, tdec, D_W), vd_all[bt:].reshape(dbatch, tdec, D_W),
        (kap, rho, cnt), DD ** -0.5)

    pc_p = h1[:bt, :C_PROJ].reshape(batch, seq, C_PROJ)
    pc_s = h1[bt:, :C_PROJ].reshape(dbatch, tdec, C_PROJ)
    shift_p = jnp.zeros((batch, 1, C_PROJ), F32)
    shift_s = state_c_shift[0][:, pc_perm][:, None, :].astype(F32)
    prev = jnp.concatenate([
        jnp.concatenate([shift_p, pc_p[:, :-1]], axis=1).reshape(bt, C_PROJ),
        jnp.concatenate([shift_s, pc_s[:, :-1]], axis=1).reshape(ds, C_PROJ)], axis=0)
    inv_perm = np.argsort(pc_perm)
    outs["c_shift_p"] = pc_p[:, -1][:, inv_perm][None]
    outs["c_shift_s"] = pc_s[:, -1][:, inv_perm][None]
    w2p = jnp.concatenate([c_w2[0], jnp.zeros((A_LORA, C_W), F32)], axis=0).astype(BF16)
    a2p = jnp.concatenate([jnp.zeros((W_LORA, C_W), F32), c_a2[0]], axis=0).astype(BF16)
    ones_bd = jnp.asarray(np.kron(np.eye(H_C, dtype=np.float32), np.ones((NC, NC), np.float32)))
    r_, w_, k_, v_, kk_, b_, g_, bonus = _rwkv_prep(
        h1, prev, c_mu[0][pc_perm], c_w0[0], c_a0[0], c_kk[0], c_ka[0], c_rk[0].reshape(-1),
        w2p, a2p, c_g2[0].astype(BF16), ones_bd)
    o_p, sc_p = _rwkv_scan(r_[:bt], w_[:bt], k_[:bt], kk_[:bt], b_[:bt], v_[:bt],
                           jnp.zeros((batch, H_C, NC, NC), F32), batch, seq)
    o_s, sc_s = _rwkv_scan(r_[bt:], w_[bt:], k_[bt:], kk_[bt:], b_[bt:], v_[bt:],
                           state_c_wkv[0].astype(F32), dbatch, tdec)
    outs["c_wkv_p"] = sc_p[None]
    outs["c_wkv_s"] = sc_s[None]
    oc = _rwkv_post(jnp.concatenate([o_p, o_s], axis=0), bonus, g_, c_ln_g[0], c_ln_b[0], ones_bd)
    mix_r = jnp.concatenate([od_p, od_s.reshape(ds, D_W)], axis=0)
    x = _outproj_ln(oc, mix_r, x, w_out_odd[0].astype(BF16), ln1_g[1], ln1_b[1])
    x = moe_block(x, 1)

    y_prompt = x[:bt].reshape(batch, seq, d)
    y_sample = x[bt:].reshape(dbatch, tdec, d)
    return (y_prompt, y_sample,
            outs["a_k_p"], outs["a_v_p"], outs["a_k_s"], outs["a_v_s"],
            outs["b_p"], outs["b_s"],
            outs["c_wkv_p"], outs["c_wkv_s"], outs["c_shift_p"], outs["c_shift_s"],
            outs["d_k_p"], outs["d_v_p"], outs["d_lf_p"], outs["d_k_s"], outs["d_v_s"], outs["d_lf_s"])
```

```python
import functools
import math

import numpy as np
import jax
import jax.numpy as jnp
from jax import lax
from jax.experimental import pallas as pl
from jax.experimental.pallas import tpu as pltpu

F32 = jnp.float32
BF16 = jnp.bfloat16
HIGHEST = lax.Precision.HIGHEST

D_MODEL = 1024
DEPTH = 2
PAGE = 128
MIX_W = D_MODEL // 2
DA = 64
DKA = 2 * DA
DVA = 2 * DA
H_A = MIX_W // DVA
DKB = 64
DVB = 128
H_B = MIX_W // DVB
RET_CHUNK = 128
NC = 64
H_C = MIX_W // NC
W_LORA = 64
A_LORA = 64
G_LORA = 128
GN_EPS_C = 64e-5
DD = 128
H_D = MIX_W // DD
N_BUCKETS = 32
MAX_DISTANCE = 128
N_EXPERTS = 16
N_GROUPS = 4
EPG = N_EXPERTS // N_GROUPS
D_EXPERT = D_MODEL // 4
DN_ALPHA = (2 * DEPTH) ** 0.25
LN_EPS = 1e-5
RMS_EPS = 1e-5
C_W = H_C * NC
C_PROJ = C_W + W_LORA + C_W + C_W + A_LORA + G_LORA
D_W = H_D * DD
EVEN_IN = 2 * H_A * DKA + H_A * DVA + 2 * H_B * DKB + 2 * H_B * DVB
ODD_IN_PAD = C_PROJ + 3 * D_W + 128

LANES = 128
SUBLANES = 8
VMEM_LIMIT = 56 * 1024 * 1024
NEG = -1e30
ATT_TQ = 256
ATT_FAR_TILES = 4
FOX_TQ = 256
FOX_FAR_TILES = 8
PAGES_PER_STEP = 16
SUFFIX_PAGES_PER_STEP = 32
SCAN_TC = 8


def _cparams(sem):
    return pltpu.CompilerParams(dimension_semantics=sem, vmem_limit_bytes=VMEM_LIMIT)


def _pick_tile(m, cap, mult=8):
    best = None
    for t in range(mult, min(m, cap) + 1, mult):
        if m % t == 0:
            best = t
    assert best is not None, (m, cap, mult)
    return best


def _layer_norm_rows(z, g, b):
    mu = jnp.mean(z, axis=-1, keepdims=True)
    zc = z - mu
    var = jnp.mean(zc * zc, axis=-1, keepdims=True)
    return zc * lax.rsqrt(var + LN_EPS) * g + b


def _sigmoid(x):
    return 1.0 / (1.0 + jnp.exp(-x))


def _softplus(y):
    return jnp.maximum(y, 0.0) + jnp.log1p(jnp.exp(-jnp.abs(y)))


def _proj_kernel(x_ref, w_ref, o_ref):
    o_ref[...] = jnp.dot(x_ref[...].astype(BF16), w_ref[...], preferred_element_type=F32)


def _proj(x, w_bf):
    m, k = x.shape
    n = w_bf.shape[1]
    tm = _pick_tile(m, 320)
    return pl.pallas_call(
        _proj_kernel,
        grid=(m // tm,),
        in_specs=[pl.BlockSpec((tm, k), lambda i: (i, 0)),
                  pl.BlockSpec((k, n), lambda i: (0, 0))],
        out_specs=pl.BlockSpec((tm, n), lambda i: (i, 0)),
        out_shape=jax.ShapeDtypeStruct((m, n), F32),
        compiler_params=_cparams(("parallel",)),
        name="proj",
    )(x, w_bf)


def _outproj_kernel(a_ref, b_ref, x_ref, w_ref, g_ref, be_ref, o_ref):
    half = a_ref.shape[1]
    y = jnp.dot(a_ref[...].astype(BF16), w_ref[0:half, :], preferred_element_type=F32)
    y = y + jnp.dot(b_ref[...].astype(BF16), w_ref[half:2 * half, :], preferred_element_type=F32)
    z = DN_ALPHA * x_ref[...] + y
    o_ref[...] = _layer_norm_rows(z, g_ref[...], be_ref[...])


def _outproj_ln(mix_l, mix_r, x, w_bf, g, b):
    m, d = x.shape
    half = mix_l.shape[1]
    tm = _pick_tile(m, 640)
    return pl.pallas_call(
        _outproj_kernel,
        grid=(m // tm,),
        in_specs=[pl.BlockSpec((tm, half), lambda i: (i, 0)),
                  pl.BlockSpec((tm, half), lambda i: (i, 0)),
                  pl.BlockSpec((tm, d), lambda i: (i, 0)),
                  pl.BlockSpec((2 * half, d), lambda i: (0, 0)),
                  pl.BlockSpec((1, d), lambda i: (0, 0)),
                  pl.BlockSpec((1, d), lambda i: (0, 0))],
        out_specs=pl.BlockSpec((tm, d), lambda i: (i, 0)),
        out_shape=jax.ShapeDtypeStruct((m, d), F32),
        compiler_params=_cparams(("parallel",)),
        name="outproj_ln",
    )(mix_l, mix_r, x, w_bf, g.reshape(1, d), b.reshape(1, d))


def _router_kernel(x_ref, wr_ref, br_ref, g_ref):
    logits = _nt_dot(wr_ref[...], x_ref[...].astype(BF16))
    mx = jnp.max(logits, axis=0, keepdims=True)
    ex = jnp.exp(logits - mx)
    scores = ex / jnp.sum(ex, axis=0, keepdims=True)
    sel = scores + br_ref[...]
    rows = [sel[e:e + 1, :] for e in range(N_EXPERTS)]
    srow = [scores[e:e + 1, :] for e in range(N_EXPERTS)]
    grp = []
    for gi in range(N_GROUPS):
        a, b, c, d = rows[EPG * gi:EPG * gi + EPG]
        hi1, lo1 = jnp.maximum(a, b), jnp.minimum(a, b)
        hi2, lo2 = jnp.maximum(c, d), jnp.minimum(c, d)
        top1 = jnp.maximum(hi1, hi2)
        top2 = jnp.maximum(jnp.minimum(hi1, hi2), jnp.maximum(lo1, lo2))
        grp.append(top1 + top2)
    best_v = grp[0]
    best_i = jnp.zeros_like(best_v, dtype=jnp.int32)
    for gi in range(1, N_GROUPS):
        better = grp[gi] > best_v
        best_v = jnp.where(better, grp[gi], best_v)
        best_i = jnp.where(better, gi, best_i)
    chosen = []
    for e in range(N_EXPERTS):
        gi = e // EPG
        rank = jnp.zeros_like(best_i)
        for j in range(EPG * gi, EPG * gi + EPG):
            if j == e:
                continue
            beats = (rows[j] >= rows[e]) if j < e else (rows[j] > rows[e])
            rank = rank + beats.astype(jnp.int32)
        chosen.append((best_i == gi) & (rank < 2))
    wsum = jnp.zeros_like(best_v)
    for e in range(N_EXPERTS):
        wsum = wsum + jnp.where(chosen[e], srow[e], 0.0)
    for e in range(N_EXPERTS):
        g_ref[e:e + 1, :] = jnp.where(chosen[e], srow[e] / wsum, 0.0)


def _router(x, w_router, b_router):
    m, d = x.shape
    tm = _pick_tile(m, 1280, LANES)
    gt = pl.pallas_call(
        _router_kernel,
        grid=(m // tm,),
        in_specs=[pl.BlockSpec((tm, d), lambda i: (i, 0)),
                  pl.BlockSpec((N_EXPERTS, d), lambda i: (0, 0)),
                  pl.BlockSpec((N_EXPERTS, 1), lambda i: (0, 0))],
        out_specs=pl.BlockSpec((N_EXPERTS, tm), lambda i: (0, i)),
        out_shape=jax.ShapeDtypeStruct((N_EXPERTS, m), F32),
        compiler_params=_cparams(("parallel",)),
        name="router",
    )(x, w_router.T.astype(BF16), b_router.reshape(N_EXPERTS, 1))
    return gt.T


def _moe_kernel(x_ref, gate_ref, wgu_ref, wd_ref, g_ref, b_ref, o_ref, xb_ref, acc_ref):
    e = pl.program_id(1)

    @pl.when(e == 0)
    def _():
        xb_ref[...] = x_ref[...].astype(BF16)
        acc_ref[...] = jnp.zeros_like(acc_ref)

    h = jnp.dot(xb_ref[...], wgu_ref[0], preferred_element_type=F32)
    gates = gate_ref[...]
    lane = lax.broadcasted_iota(jnp.int32, gates.shape, 1)
    gcol = jnp.sum(jnp.where(lane == e, gates, 0.0), axis=1, keepdims=True)
    hg = h[:, :D_EXPERT]
    hu = h[:, D_EXPERT:]
    act = hg * _sigmoid(hg) * hu * gcol
    acc_ref[...] += jnp.dot(act.astype(BF16), wd_ref[0], preferred_element_type=F32)

    @pl.when(e == N_EXPERTS - 1)
    def _():
        z = DN_ALPHA * x_ref[...] + acc_ref[...]
        o_ref[...] = _layer_norm_rows(z, g_ref[...], b_ref[...])


def _moe_ln(x, gates, wgu_bf, wd_bf, g, b):
    m, d = x.shape
    tm = _pick_tile(m, 640)
    return pl.pallas_call(
        _moe_kernel,
        grid=(m // tm, N_EXPERTS),
        in_specs=[pl.BlockSpec((tm, d), lambda i, e: (i, 0)),
                  pl.BlockSpec((tm, N_EXPERTS), lambda i, e: (i, 0)),
                  pl.BlockSpec((1, d, 2 * D_EXPERT), lambda i, e: (e, 0, 0)),
                  pl.BlockSpec((1, D_EXPERT, d), lambda i, e: (e, 0, 0)),
                  pl.BlockSpec((1, d), lambda i, e: (0, 0)),
                  pl.BlockSpec((1, d), lambda i, e: (0, 0))],
        out_specs=pl.BlockSpec((tm, d), lambda i, e: (i, 0)),
        out_shape=jax.ShapeDtypeStruct((m, d), F32),
        scratch_shapes=[pltpu.VMEM((tm, d), BF16), pltpu.VMEM((tm, d), F32)],
        compiler_params=_cparams(("parallel", "arbitrary")),
        name="moe_ln",
    )(x, gates, wgu_bf, wd_bf, g.reshape(1, d), b.reshape(1, d))


def _softmax_update(s, v_bf, m_ref, l_ref, acc_ref, idx):
    m_prev = m_ref[idx][:, 0:1]
    l_prev = l_ref[idx][:, 0:1]
    m_new = jnp.maximum(m_prev, jnp.max(s, axis=1, keepdims=True))
    alpha = jnp.exp(m_prev - m_new)
    p = jnp.exp(s - m_new)
    l_new = alpha * l_prev + jnp.sum(p, axis=1, keepdims=True)
    acc_ref[idx] = alpha * acc_ref[idx] + jnp.dot(p.astype(BF16), v_bf, preferred_element_type=F32)
    m_ref[idx] = jnp.broadcast_to(m_new, m_ref.shape[1:])
    l_ref[idx] = jnp.broadcast_to(l_new, l_ref.shape[1:])


def _softmax_init(m_ref, l_ref, acc_ref):
    m_ref[...] = jnp.full(m_ref.shape, NEG, F32)
    l_ref[...] = jnp.zeros(l_ref.shape, F32)
    acc_ref[...] = jnp.zeros(acc_ref.shape, F32)


def _far_loops(nfar, body, far_tiles):
    nwide = nfar // far_tiles
    npair = nwide // 2

    def wide_pair(j, carry):
        body(2 * j, far_tiles)
        body(2 * j + 1, far_tiles)
        return carry

    def wide(j, carry):
        body(j, far_tiles)
        return carry

    def single(j, carry):
        body(j, 1)
        return carry

    lax.fori_loop(0, npair, wide_pair, 0)
    lax.fori_loop(2 * npair, nwide, wide, 0)
    done = nwide * far_tiles
    half = far_tiles // 2
    if half > 1:
        take_half = (nfar - done) >= half

        @pl.when(take_half)
        def _():
            body(done // half, half)

        done = done + jnp.where(take_half, half, 0)
    lax.fori_loop(done, nfar, single, 0)


def _nt_dot(a, b):
    return lax.dot_general(a, b, (((1,), (1,)), ((), ())), preferred_element_type=F32)


def _tn_dot(a, b):
    return lax.dot_general(a, b, (((0,), (0,)), ((), ())), preferred_element_type=F32)


def _t5_bucket_np(dist):
    max_exact = N_BUCKETS // 2
    d = np.maximum(dist, 0)
    ratio = (np.maximum(d, 1).astype(np.float32) / np.float32(max_exact)).astype(np.float32)
    log_ratio = (np.log(ratio).astype(np.float32) / np.float32(math.log(MAX_DISTANCE / max_exact))).astype(np.float32)
    large = np.minimum(max_exact + (log_ratio * np.float32(N_BUCKETS - max_exact)).astype(np.int32), N_BUCKETS - 1)
    return np.where(d < max_exact, d, large).astype(np.int32)


_T5_FAR = int(np.min(np.nonzero(_t5_bucket_np(np.arange(4 * MAX_DISTANCE)) == N_BUCKETS - 1)[0]))
assert np.all(_t5_bucket_np(np.arange(_T5_FAR, 8 * MAX_DISTANCE)) == N_BUCKETS - 1)


def _lam_from_params(lp):
    s01 = jnp.sum(lp[0:1, :] * lp[1:2, :], axis=1, keepdims=True)
    s23 = jnp.sum(lp[2:3, :] * lp[3:4, :], axis=1, keepdims=True)
    return jnp.exp(s01) - jnp.exp(s23)


def _diff_finalize(o0, o1, lam, gain, lam_init):
    o = o0 - lam * o1
    ms = jnp.mean(o * o, axis=-1, keepdims=True)
    return o * lax.rsqrt(ms + RMS_EPS) * gain * (1.0 - lam_init)


def _diff_prompt_kernel(q_ref, k_ref, v_ref, tab_ref, lp_ref, gain_ref, o_ref,
                        kb_ref, vb_ref, m_ref, l_ref, acc_ref, *, tq, lam_init):
    i = pl.program_id(2)

    @pl.when(i == 0)
    def _():
        kb_ref[...] = k_ref[...].astype(BF16)
        vb_ref[...] = v_ref[...].astype(BF16)

    _softmax_init(m_ref, l_ref, acc_ref)
    qf = q_ref[...] * (DA ** -0.5)
    lane = lax.broadcasted_iota(jnp.int32, qf.shape, 1)
    qm = (jnp.where(lane < DA, qf, 0.0).astype(BF16), jnp.where(lane >= DA, qf, 0.0).astype(BF16))

    def chunk_at(ks, width, bias):
        kc = kb_ref[pl.ds(ks, width), :]
        vc = vb_ref[pl.ds(ks, width), :]
        for mi in range(2):
            s = _nt_dot(qm[mi], kc)
            if bias is not None:
                s = s + bias
            _softmax_update(s, vc, m_ref, l_ref, acc_ref, mi)

    def far_chunk(j, tiles):
        width = tiles * tq
        chunk_at(pl.multiple_of(j * width, width), width, None)

    _far_loops(jnp.maximum(i - 1, 0), far_chunk, ATT_FAR_TILES)

    @pl.when(i >= 1)
    def _():
        chunk_at(pl.multiple_of((i - 1) * tq, tq), 2 * tq, jnp.concatenate([tab_ref[0, 1], tab_ref[0, 0]], axis=1))

    @pl.when(i == 0)
    def _():
        chunk_at(0, tq, tab_ref[0, 0])
    lam = _lam_from_params(lp_ref[...]) + lam_init
    o0 = acc_ref[0] / l_ref[0][:, 0:1]
    o1 = acc_ref[1] / l_ref[1][:, 0:1]
    o_ref[...] = _diff_finalize(o0, o1, lam, gain_ref[...], lam_init)


def _diff_tables(rel_bias, tq):
    qi = np.arange(tq)[:, None]
    kj = np.arange(tq)[None, :]
    rb = rel_bias.astype(F32)
    rel = rb - rb[N_BUCKETS - 1][None, :]
    def lookup(bucket):
        onehot = jnp.asarray(bucket)[..., None] == jnp.arange(N_BUCKETS)
        return jnp.sum(jnp.where(onehot[..., None], rel[None, None], 0.0), axis=2)

    diag = jnp.where(jnp.asarray(qi >= kj)[..., None], lookup(_t5_bucket_np(qi - kj)), NEG)
    prev = lookup(_t5_bucket_np(tq + qi - kj))
    return jnp.moveaxis(jnp.stack([diag, prev], 0), -1, 0)


def _diff_prompt(h0, batch, seq, rel_bias, lam_params, subln_g, lam_init):
    tq = min(ATT_TQ, seq)
    assert seq % tq == 0 and tq + 1 >= _T5_FAR
    nq = seq // tq
    tab = _diff_tables(rel_bias, tq)
    kern = functools.partial(_diff_prompt_kernel, tq=tq, lam_init=lam_init)
    qoff, koff, voff = 0, H_A, 2 * H_A
    return pl.pallas_call(
        kern,
        grid=(batch, H_A, nq),
        in_specs=[pl.BlockSpec((tq, DKA), lambda b, h, i: (b * nq + i, qoff + h)),
                  pl.BlockSpec((seq, DKA), lambda b, h, i: (b, koff + h)),
                  pl.BlockSpec((seq, DVA), lambda b, h, i: (b, voff + h)),
                  pl.BlockSpec((1, 2, tq, tq), lambda b, h, i: (h, 0, 0, 0)),
                  pl.BlockSpec((4, DA), lambda b, h, i: (0, 0)),
                  pl.BlockSpec((1, DVA), lambda b, h, i: (0, 0))],
        out_specs=pl.BlockSpec((tq, DVA), lambda b, h, i: (b * nq + i, h)),
        out_shape=jax.ShapeDtypeStruct((batch * seq, H_A * DVA), F32),
        scratch_shapes=[pltpu.VMEM((seq, DKA), BF16), pltpu.VMEM((seq, DVA), BF16),
                        pltpu.VMEM((2, tq, LANES), F32), pltpu.VMEM((2, tq, LANES), F32),
                        pltpu.VMEM((2, tq, DVA), F32)],
        compiler_params=_cparams(("parallel", "parallel", "arbitrary")),
        name="diff_prompt",
    )(h0, h0, h0, tab, lam_params, subln_g.reshape(1, DVA))


def _fox_prompt_kernel(q_ref, k_ref, v_ref, c_ref, kap_ref, o_ref,
                       kb_ref, vb_ref, m_ref, l_ref, acc_ref, *, tq):
    h = pl.program_id(1)
    i = pl.program_id(2)

    @pl.when(i == 0)
    def _():
        kb_ref[...] = k_ref[...].astype(BF16)
        vb_ref[...] = v_ref[...].astype(BF16)

    _softmax_init(m_ref, l_ref, acc_ref)
    qb = q_ref[...].astype(BF16)
    cblk = c_ref[...]
    lane = lax.broadcasted_iota(jnp.int32, cblk.shape, 1)
    rho = jnp.sum(jnp.where(lane == h, cblk, 0.0), axis=1, keepdims=True)

    def chunk(j, masked, tiles=1):
        width = tiles * tq
        ks = pl.multiple_of(j * width, width)
        kc = kb_ref[pl.ds(ks, width), :]
        vc = vb_ref[pl.ds(ks, width), :]
        s = _nt_dot(qb, kc) * (DD ** -0.5) + (rho - kap_ref[0, :, pl.ds(ks, width)])
        if masked:
            r = lax.broadcasted_iota(jnp.int32, s.shape, 0)
            c = lax.broadcasted_iota(jnp.int32, s.shape, 1)
            s = jnp.where(r >= c, s, NEG)
        _softmax_update(s, vc, m_ref, l_ref, acc_ref, 0)

    _far_loops(i, lambda j, tiles: chunk(j, False, tiles), FOX_FAR_TILES)
    chunk(i, True)
    o_ref[...] = acc_ref[0] / l_ref[0][:, 0:1]


def _fox_prompt(h1, c_all, batch, seq):
    tq = min(FOX_TQ, seq)
    assert seq % tq == 0
    nq = seq // tq
    qoff = C_PROJ // LANES
    koff = qoff + H_D
    voff = koff + H_D
    kap = c_all[:batch * seq, :H_D].reshape(batch, seq, H_D).transpose(0, 2, 1).reshape(batch * H_D, 1, seq)
    kern = functools.partial(_fox_prompt_kernel, tq=tq)
    return pl.pallas_call(
        kern,
        grid=(batch, H_D, nq),
        in_specs=[pl.BlockSpec((tq, DD), lambda b, h, i: (b * nq + i, qoff + h)),
                  pl.BlockSpec((seq, DD), lambda b, h, i: (b, koff + h)),
                  pl.BlockSpec((seq, DD), lambda b, h, i: (b, voff + h)),
                  pl.BlockSpec((tq, LANES), lambda b, h, i: (b * nq + i, 0)),
                  pl.BlockSpec((1, 1, seq), lambda b, h, i: (b * H_D + h, 0, 0))],
        out_specs=pl.BlockSpec((tq, DD), lambda b, h, i: (b * nq + i, h)),
        out_shape=jax.ShapeDtypeStruct((batch * seq, D_W), F32),
        scratch_shapes=[pltpu.VMEM((seq, DD), BF16), pltpu.VMEM((seq, DD), BF16),
                        pltpu.VMEM((1, tq, LANES), F32), pltpu.VMEM((1, tq, LANES), F32),
                        pltpu.VMEM((1, tq, DD), F32)],
        compiler_params=_cparams(("parallel", "parallel", "arbitrary")),
        name="fox_prompt",
    )(h1, h1, h1, c_all, kap)


def _fox_prep_kernel(fd_ref, bf_ref, tri_ref, lf_ref, c_ref, carry_ref):
    @pl.when(pl.program_id(1) == 0)
    def _():
        carry_ref[...] = jnp.zeros_like(carry_ref)

    x = fd_ref[...] + bf_ref[...]
    lf = -_softplus(-x)
    inc = jnp.dot(tri_ref[...], lf, precision=HIGHEST, preferred_element_type=F32) + carry_ref[0:1, :]
    lf_ref[...] = lf
    c_ref[...] = inc
    tm = inc.shape[0]
    carry_ref[...] = jnp.broadcast_to(inc[tm - 1:tm, :], carry_ref.shape)


def _fox_prep(h1, bf_pad, row0, batch, seq):
    tm = _pick_tile(seq, 256)
    nt = seq // tm
    assert row0 % tm == 0
    blk0 = row0 // tm
    col = (ODD_IN_PAD - LANES) // LANES
    tri = jnp.asarray(np.tril(np.ones((tm, tm), np.float32)))
    return pl.pallas_call(
        _fox_prep_kernel,
        grid=(batch, nt),
        in_specs=[pl.BlockSpec((tm, LANES), lambda b, i: (blk0 + b * nt + i, col)),
                  pl.BlockSpec((1, LANES), lambda b, i: (0, 0)),
                  pl.BlockSpec((tm, tm), lambda b, i: (0, 0))],
        out_specs=[pl.BlockSpec((tm, LANES), lambda b, i: (b * nt + i, 0)),
                   pl.BlockSpec((tm, LANES), lambda b, i: (b * nt + i, 0))],
        out_shape=[jax.ShapeDtypeStruct((batch * seq, LANES), F32),
                   jax.ShapeDtypeStruct((batch * seq, LANES), F32)],
        scratch_shapes=[pltpu.VMEM((8, LANES), F32)],
        compiler_params=_cparams(("parallel", "arbitrary")),
        name="fox_prep",
    )(h1, bf_pad, tri)


def _suffix_kernel(pt_ref, *refs, gpages, nheads):
    lf_refs = refs[:gpages]
    ut_ref, later_ref, o_ref, carry_ref = refs[gpages:]
    hp = carry_ref.shape[0]

    @pl.when(pl.program_id(1) == 0)
    def _():
        carry_ref[...] = jnp.zeros_like(carry_ref)

    lf = jnp.concatenate([r[...] for r in lf_refs], axis=0)
    incl = jnp.dot(lf, ut_ref[...], precision=HIGHEST, preferred_element_type=F32)
    totals = jnp.broadcast_to(incl[:, 0:1], incl.shape)
    later = jnp.dot(later_ref[...], totals, precision=HIGHEST, preferred_element_type=F32)
    carry = carry_ref[...]
    out = incl - lf + later + jnp.concatenate([carry] * gpages, axis=0)
    for j in range(gpages):
        o_ref[0, :, j * PAGE:(j + 1) * PAGE] = out[j * hp:j * hp + nheads, :]
    carry_ref[...] = carry + later[0:hp] + totals[0:hp]


def _suffix_sums(logf_pool_t, page_table_flat, dbatch, n_pages):
    gp = _pick_tile(n_pages, SUFFIX_PAGES_PER_STEP, 1)
    ng = n_pages // gp
    nheads = logf_pool_t.shape[1]
    hp = -(-nheads // SUBLANES) * SUBLANES
    pool = jnp.pad(logf_pool_t, ((0, 0), (0, hp - nheads), (0, 0)))
    ut = jnp.asarray(np.tril(np.ones((PAGE, PAGE), np.float32)))
    later = jnp.asarray(np.kron(np.triu(np.ones((gp, gp), np.float32), 1), np.eye(hp, dtype=np.float32)))

    def page_map(j):
        return lambda b, g, pt: (pt[b * n_pages + (ng - 1 - g) * gp + j], 0, 0)

    in_specs = [pl.BlockSpec((None, hp, PAGE), page_map(j)) for j in range(gp)]
    in_specs.append(pl.BlockSpec((PAGE, PAGE), lambda b, g, pt: (0, 0)))
    in_specs.append(pl.BlockSpec((gp * hp, gp * hp), lambda b, g, pt: (0, 0)))
    return pl.pallas_call(
        functools.partial(_suffix_kernel, gpages=gp, nheads=nheads),
        grid_spec=pltpu.PrefetchScalarGridSpec(
            num_scalar_prefetch=1,
            grid=(dbatch, ng),
            in_specs=in_specs,
            out_specs=pl.BlockSpec((1, nheads, gp * PAGE), lambda b, g, pt: (b, 0, ng - 1 - g)),
            scratch_shapes=[pltpu.VMEM((hp, LANES), F32)]),
        out_shape=jax.ShapeDtypeStruct((dbatch, nheads, n_pages * PAGE), F32),
        compiler_params=_cparams(("parallel", "arbitrary")),
        name="fox_suffix",
    )(page_table_flat, *([pool] * gp), ut, later)


def _paged_kernel(pt_ref, *refs, gpages, tdec, mode, scale, lam_init, chunk, nheads):
    q_ref = refs[0]
    k_refs = refs[1:1 + gpages]
    v_refs = refs[1 + gpages:1 + 2 * gpages]
    rest = refs[1 + 2 * gpages:]
    if mode == "diff":
        tab_ref, tself_ref, kn_ref, vn_ref, lp_ref, gain_ref, o_ref, s_ref, vb_ref = rest
    else:
        kap_ref, rho_ref, cnt_ref, kn_ref, vn_ref, o_ref, s_ref, vb_ref = rest
    g = pl.program_id(1)
    ng = pl.num_programs(1)
    gw = gpages * PAGE
    rows, past = s_ref.shape
    rph = rows // nheads
    dh = q_ref.shape[2]
    off = pl.multiple_of(g * gw, gw)

    def head_rows(ref, h):
        return ref[pl.ds(h, PAGE, stride=nheads), :]

    for h in range(nheads):
        hs = slice(h * rph, (h + 1) * rph)
        kh = jnp.concatenate([head_rows(r, h).astype(BF16) for r in k_refs], axis=0)
        s = _nt_dot(q_ref[0, hs, :], kh)
        if scale != 1.0:
            s = s * scale
        if mode == "diff":
            s = s + tab_ref[0, hs, :]
        else:
            s = (s + rho_ref[0, hs, :]) + kap_ref[0, h:h + 1, :]
        s_ref[hs, pl.ds(off, gw)] = s
        for j in range(gpages):
            vb_ref[h, pl.ds(off + j * PAGE, PAGE), :] = head_rows(v_refs[j], h).astype(BF16)

    @pl.when(g == ng - 1)
    def _():
        tails = []
        for h in range(nheads):
            hs = slice(h * rph, (h + 1) * rph)
            kn = kn_ref[0, :, h * dh:(h + 1) * dh].astype(BF16)
            t = _nt_dot(q_ref[0, hs, :], kn)
            if scale != 1.0:
                t = t * scale
            if mode == "diff":
                t = t + tself_ref[hs, :]
            else:
                t = (t + rho_ref[0, hs, :]) - cnt_ref[0, h:h + 1, :]
                r = lax.broadcasted_iota(jnp.int32, t.shape, 0)
                c = lax.broadcasted_iota(jnp.int32, t.shape, 1)
                t = jnp.where(c <= r, t, NEG)
            tails.append(t)
        tail = jnp.concatenate(tails, axis=0)
        if mode == "diff":
            lam = _lam_from_params(lp_ref[...]) + lam_init
        nchunk = past // chunk

        def sc(ci, hs=slice(None)):
            return s_ref[hs, pl.ds(pl.multiple_of(ci * chunk, chunk), chunk)]

        m = lax.fori_loop(0, nchunk, lambda ci, mm: jnp.maximum(mm, jnp.max(sc(ci), axis=1, keepdims=True)),
                          jnp.max(tail, axis=1, keepdims=True))
        den = lax.fori_loop(0, nchunk, lambda ci, dd: dd + jnp.sum(jnp.exp(sc(ci) - m), axis=1, keepdims=True),
                            jnp.sum(jnp.exp(tail - m), axis=1, keepdims=True))

        def weights(sv, h):
            hs = slice(h * rph, (h + 1) * rph)
            p = jnp.exp(sv - m[hs]) / den[hs]
            if mode == "diff":
                p = p[0:tdec] - lam * p[tdec:2 * tdec]
            return p.astype(BF16)

        def pv_body(ci, accs):
            cs = pl.multiple_of(ci * chunk, chunk)
            return tuple(
                accs[h] + jnp.dot(weights(sc(ci, slice(h * rph, (h + 1) * rph)), h), vb_ref[h, pl.ds(cs, chunk), :],
                                  preferred_element_type=F32)
                for h in range(nheads))

        init = tuple(
            jnp.dot(weights(tails[h], h), vn_ref[0, :, h * dh:(h + 1) * dh].astype(BF16), preferred_element_type=F32)
            for h in range(nheads))
        outs = lax.fori_loop(0, nchunk, pv_body, init)
        for h in range(nheads):
            oh = outs[h]
            if mode == "diff":
                ms = jnp.mean(oh * oh, axis=-1, keepdims=True)
                oh = oh * lax.rsqrt(ms + RMS_EPS) * gain_ref[...] * (1.0 - lam_init)
            o_ref[0, :, h * dh:(h + 1) * dh] = oh


def _query_rows(q, n_maps):
    db, t, nh, w = q.shape
    dm = w // n_maps
    qh = q.transpose(0, 2, 1, 3)
    sel = jnp.asarray(np.kron(np.eye(n_maps, dtype=np.float32), np.ones((1, dm), np.float32)))
    rows = qh[:, :, None, :, :] * sel[None, None, :, None, :]
    return rows.reshape(db, nh * n_maps * t, w)


def _paged_attention(mode, q_rows_bf, k_pool, v_pool, page_table_flat, n_pages, k_new, v_new, extra, scale,
                     lam_init=0.0):
    dbatch, rows, dh = q_rows_bf.shape
    tdec = k_new.shape[1]
    width = k_new.shape[2]
    nheads = width // dh
    gp = _pick_tile(n_pages, PAGES_PER_STEP, 1)
    ng = n_pages // gp
    past = n_pages * PAGE
    chunk = _pick_tile(past, 2048, LANES)
    pad_rows = lambda x: jnp.pad(x, ((0, 0), (0, PAGE - tdec), (0, 0)))
    k_new = pad_rows(k_new)
    v_new = pad_rows(v_new)

    def page_map(j):
        return lambda b, g, pt: (pt[b * n_pages + g * gp + j], 0, 0)

    in_specs = [pl.BlockSpec((1, rows, dh), lambda b, g, pt: (b, 0, 0))]
    in_specs += [pl.BlockSpec((None, PAGE * nheads, dh), page_map(j)) for j in range(gp)]
    in_specs += [pl.BlockSpec((None, PAGE * nheads, dh), page_map(j)) for j in range(gp)]
    new_spec = pl.BlockSpec((1, PAGE, width), lambda b, g, pt: (b, 0, 0))
    if mode == "diff":
        tab, tself, lam_params, gain = extra
        tself = jnp.pad(tself, ((0, 0), (0, PAGE - tdec)), constant_values=NEG)
        in_specs += [pl.BlockSpec((1, rows, gp * PAGE), lambda b, g, pt: ((g + 1) // ng, 0, 0)),
                     pl.BlockSpec((rows, PAGE), lambda b, g, pt: (0, 0)),
                     new_spec, new_spec,
                     pl.BlockSpec((4, DA), lambda b, g, pt: (0, 0)),
                     pl.BlockSpec((1, DVA), lambda b, g, pt: (0, 0))]
        operands = [tab, tself, k_new, v_new, lam_params, gain]
    else:
        kap, rho, cnt = extra
        cnt = jnp.pad(cnt, ((0, 0), (0, 0), (0, PAGE - tdec)))
        in_specs += [pl.BlockSpec((1, H_D, gp * PAGE), lambda b, g, pt: (b, 0, g)),
                     pl.BlockSpec((1, rows, 1), lambda b, g, pt: (b, 0, 0)),
                     pl.BlockSpec((1, H_D, PAGE), lambda b, g, pt: (b, 0, 0)),
                     new_spec, new_spec]
        operands = [kap, rho, cnt, k_new, v_new]
    kern = functools.partial(_paged_kernel, gpages=gp, tdec=tdec, mode=mode, scale=scale, lam_init=lam_init,
                             chunk=chunk, nheads=nheads)
    return pl.pallas_call(
        kern,
        grid_spec=pltpu.PrefetchScalarGridSpec(
            num_scalar_prefetch=1,
            grid=(dbatch, ng),
            in_specs=in_specs,
            out_specs=pl.BlockSpec((1, tdec, width), lambda b, g, pt: (b, 0, 0)),
            scratch_shapes=[pltpu.VMEM((rows, past), F32), pltpu.VMEM((nheads, past, dh), BF16)]),
        out_shape=jax.ShapeDtypeStruct((dbatch, tdec, width), F32),
        compiler_params=_cparams(("parallel", "arbitrary")),
        name="paged_" + mode,
    )(page_table_flat, q_rows_bf, *([k_pool] * gp), *([v_pool] * gp), *operands)


def _diff_decode_tables(rel_bias, n_pages, gp, tdec):
    rb = rel_bias.astype(F32)
    rel = (rb - rb[N_BUCKETS - 1][None, :]).T
    past = n_pages * PAGE
    t = np.arange(tdec)[:, None]
    kpos = past - gp * PAGE + np.arange(gp * PAGE)[None, :]
    assert PAGE + 1 >= _T5_FAR
    last = rel[:, _t5_bucket_np(past + t - kpos)]
    s = np.arange(tdec)[None, :]
    selfb = jnp.where(jnp.asarray(t >= s)[None], rel[:, _t5_bucket_np(t - s)], NEG)
    rep = lambda x: jnp.broadcast_to(x[:, None], (H_A, 2) + x.shape[1:]).reshape((H_A * 2 * tdec,) + x.shape[2:])
    last_r = rep(last)
    tab = jnp.stack([jnp.zeros_like(last_r), last_r], 0)
    return tab, rep(selfb)


def _retention_kernel(q_ref, k_ref, v_ref, gb_ref, cos_ref, sin_ref, qdec_ref, kdec_ref, dmat_ref,
                      gc_ref, bd_ref, s0_ref, o_ref, sout_ref, s_ref):
    c = pl.program_id(1)

    @pl.when(c == 0)
    def _():
        s_ref[...] = s0_ref[0]

    cos = cos_ref[...]
    sin = sin_ref[...]
    width = cos.shape[1]
    lane = lax.broadcasted_iota(jnp.int32, cos.shape, 1)
    first_half = (lane % DKB) < (DKB // 2)

    def rot(x):
        swapped = jnp.where(first_half, -pltpu.roll(x, width - DKB // 2, 1), pltpu.roll(x, DKB // 2, 1))
        return x * cos + swapped * sin

    qr = rot(q_ref[...])
    kr = rot(k_ref[...]) * (DKB ** -0.5)
    v = v_ref[...]
    vb = v.astype(BF16)
    state = s_ref[...]
    cross = jnp.dot((qr * qdec_ref[...]).astype(BF16), state.astype(BF16), preferred_element_type=F32)
    lane_pair = lax.broadcasted_iota(jnp.int32, (qr.shape[0], LANES), 1)
    gate = gb_ref[...]
    for h in range(H_B):
        cb = (h * DKB) // LANES
        lo = (h * DKB) % LANES
        qh = qr[:, cb * LANES:(cb + 1) * LANES]
        qh = jnp.where((lane_pair >= lo) & (lane_pair < lo + DKB), qh, 0.0).astype(BF16)
        kh = kr[:, cb * LANES:(cb + 1) * LANES].astype(BF16)
        att = _nt_dot(qh, kh) * dmat_ref[h]
        oh = jnp.dot(att.astype(BF16), vb[:, h * DVB:(h + 1) * DVB], preferred_element_type=F32)
        oh = oh + cross[:, h * DVB:(h + 1) * DVB]
        ms = jnp.mean(oh * oh, axis=-1, keepdims=True)
        gh = gate[:, h * DVB:(h + 1) * DVB]
        o_ref[:, h * DVB:(h + 1) * DVB] = oh * lax.rsqrt(ms + RMS_EPS) * (gh * _sigmoid(gh))
    upd = _tn_dot((kr * kdec_ref[...]).astype(BF16), vb)
    new_state = state * gc_ref[...] + upd * bd_ref[...]
    s_ref[...] = new_state

    @pl.when(c == pl.num_programs(1) - 1)
    def _():
        sout_ref[0] = new_state


def _retention(h0, row0, batch, seq, start, s0_bd):
    ch = RET_CHUNK if seq % RET_CHUNK == 0 else seq
    n = seq // ch
    assert row0 % ch == 0
    blk0 = row0 // ch
    kw = H_B * DKB
    vw = H_B * DVB
    qcol = (2 * H_A * DKA + H_A * DVA) // kw
    kcol = qcol + 1
    vcol = (2 * H_A * DKA + H_A * DVA + 2 * kw) // vw
    gcol = vcol + 1
    half = DKB // 2
    inv = 1.0 / (10000.0 ** (jnp.arange(half, dtype=F32) / half))
    pos = (start + jnp.arange(seq)).astype(F32)
    ang = pos[:, None] * inv[None, :]
    cos = jnp.tile(jnp.cos(ang), (1, 2 * H_B))
    sin = jnp.tile(jnp.sin(ang), (1, 2 * H_B))
    log_g = jnp.log(1.0 - jnp.exp2(-5.0 - jnp.arange(H_B, dtype=F32)))
    idx = jnp.arange(ch, dtype=F32)
    diff = idx[:, None] - idx[None, :]
    dmat = jnp.where(diff >= 0, jnp.exp(log_g[:, None, None] * jnp.maximum(diff, 0.0)), 0.0)
    qdec = jnp.repeat(jnp.exp(log_g[None, :] * (idx[:, None] + 1.0)), DKB, axis=1)
    kdec = jnp.repeat(jnp.exp(log_g[None, :] * (ch - 1.0 - idx[:, None])), DKB, axis=1)
    gc = jnp.broadcast_to(jnp.repeat(jnp.exp(log_g * ch), DKB)[:, None], (kw, vw))
    bd = jnp.asarray(np.kron(np.eye(H_B, dtype=np.float32), np.ones((DKB, DVB), np.float32)))
    out, s_fin = pl.pallas_call(
        _retention_kernel,
        grid=(batch, n),
        in_specs=[pl.BlockSpec((ch, kw), lambda b, c: (blk0 + b * n + c, qcol)),
                  pl.BlockSpec((ch, kw), lambda b, c: (blk0 + b * n + c, kcol)),
                  pl.BlockSpec((ch, vw), lambda b, c: (blk0 + b * n + c, vcol)),
                  pl.BlockSpec((ch, vw), lambda b, c: (blk0 + b * n + c, gcol)),
                  pl.BlockSpec((ch, kw), lambda b, c: (c, 0)),
                  pl.BlockSpec((ch, kw), lambda b, c: (c, 0)),
                  pl.BlockSpec((ch, kw), lambda b, c: (0, 0)),
                  pl.BlockSpec((ch, kw), lambda b, c: (0, 0)),
                  pl.BlockSpec((H_B, ch, ch), lambda b, c: (0, 0, 0)),
                  pl.BlockSpec((kw, vw), lambda b, c: (0, 0)),
                  pl.BlockSpec((kw, vw), lambda b, c: (0, 0)),
                  pl.BlockSpec((1, kw, vw), lambda b, c: (b, 0, 0))],
        out_specs=[pl.BlockSpec((ch, vw), lambda b, c: (b * n + c, 0)),
                   pl.BlockSpec((1, kw, vw), lambda b, c: (b, 0, 0))],
        out_shape=[jax.ShapeDtypeStruct((batch * seq, vw), F32),
                   jax.ShapeDtypeStruct((batch, kw, vw), F32)],
        scratch_shapes=[pltpu.VMEM((kw, vw), F32)],
        compiler_params=_cparams(("parallel", "arbitrary")),
        name="retention",
    )(h0, h0, h0, h0, cos, sin, qdec, kdec, dmat, gc, bd, s0_bd)
    return out, s_fin


def _state_to_bd(s):
    b = s.shape[0]
    eye = jnp.eye(H_B, dtype=s.dtype)
    return (s[:, :, :, None, :] * eye[None, :, None, :, None]).reshape(b, H_B * DKB, H_B * DVB)


def _bd_to_state(sbd):
    b = sbd.shape[0]
    s5 = sbd.reshape(b, H_B, DKB, H_B, DVB)
    return jnp.stack([s5[:, h, :, h, :] for h in range(H_B)], axis=1)


def _rwkv_prep_kernel(pc_ref, prev_ref, mu_ref, w0_ref, a0_ref, kkw_ref, kaw_ref, rk_ref,
                      w2_ref, a2_ref, g2_ref, ones_ref,
                      r_ref, w_ref, k_ref, v_ref, kk_ref, b_ref, g_ref, bonus_ref):
    pc = pc_ref[...]
    pm = pc + (prev_ref[...] - pc) * mu_ref[...]
    r = pm[:, 0:C_W]
    k = pm[:, C_W:2 * C_W]
    v = pm[:, 2 * C_W:3 * C_W]
    wa = pm[:, 3 * C_W:3 * C_W + W_LORA + A_LORA]
    gd = pm[:, 3 * C_W + W_LORA + A_LORA:]
    wl = jnp.dot(jnp.tanh(wa).astype(BF16), w2_ref[...], preferred_element_type=F32)
    w_log = -_softplus(-(w0_ref[...] + wl)) - 0.5
    decay = jnp.exp(-jnp.exp(w_log))
    a = _sigmoid(a0_ref[...] + jnp.dot(wa.astype(BF16), a2_ref[...], preferred_element_type=F32))
    g = jnp.dot(_sigmoid(gd).astype(BF16), g2_ref[...], preferred_element_type=F32)

    def seg_sum(x):
        return jnp.dot(x, ones_ref[...], precision=HIGHEST, preferred_element_type=F32)

    kkr = k * kkw_ref[...]
    kk = kkr / jnp.maximum(jnp.sqrt(seg_sum(kkr * kkr)), 1e-12)
    k2 = k * (1.0 + (a - 1.0) * kaw_ref[...])
    r_ref[...] = r
    w_ref[...] = decay
    k_ref[...] = k2
    v_ref[...] = v
    kk_ref[...] = kk
    b_ref[...] = kk * a
    g_ref[...] = g
    bonus_ref[...] = seg_sum(r * k2 * rk_ref[...]) * v


def _rwkv_prep(h1, prev, mu_p, w0, a0, kkw, kaw, rk, w2p, a2p, g2, ones_bd):
    m = h1.shape[0]
    tm = _pick_tile(m, 320)
    row = lambda x: x.reshape(1, -1)
    tok = lambda w: pl.BlockSpec((tm, w), lambda i: (i, 0))
    full = lambda a: pl.BlockSpec(a.shape, lambda i: (0, 0))
    small = [row(mu_p), row(w0), row(a0), row(kkw), row(kaw), row(rk), w2p, a2p, g2, ones_bd]
    outs = pl.pallas_call(
        _rwkv_prep_kernel,
        grid=(m // tm,),
        in_specs=[tok(C_PROJ), tok(C_PROJ)] + [full(a) for a in small],
        out_specs=[tok(C_W)] * 8,
        out_shape=[jax.ShapeDtypeStruct((m, C_W), F32)] * 8,
        compiler_params=_cparams(("parallel",)),
        name="rwkv_prep",
    )(h1, prev, *small)
    return outs


def _split_bf16(x):
    hi = x.astype(BF16)
    return hi, (x - hi.astype(F32)).astype(BF16)


def _dot3(a, b, dot, axis):
    ah, al = _split_bf16(a)
    bh, bl = _split_bf16(b)
    return dot(jnp.concatenate([ah, ah, al], axis=axis), jnp.concatenate([bh, bl, bh], axis=axis))


def _rwkv_scan_kernel(r_ref, w_ref, k_ref, kk_ref, b_ref, v_ref, s0_ref, o_ref, sout_ref, s_ref, *, tc):
    tb = pl.program_id(1)

    @pl.when(tb == 0)
    def _():
        s_ref[...] = s0_ref[...]

    nb, npair = s_ref.shape[0], s_ref.shape[1]
    pairs = [(bi, p) for bi in range(nb) for p in range(npair)]
    g = len(pairs)
    pw = 2 * NC
    shape = (g, NC, pw)
    lo = lax.broadcasted_iota(jnp.int32, shape, 2) < NC
    lo_rows = lax.broadcasted_iota(jnp.int32, (tc, pw), 1) < NC
    step_of_row = lax.broadcasted_iota(jnp.int32, (2 * tc, pw), 0) % tc

    def rows(ref, t):
        return jnp.concatenate([ref[bi, t:t + 1, p * pw:(p + 1) * pw][None] for bi, p in pairs], axis=0)

    def half_sums(x):
        even = jnp.sum(jnp.where(lo, x, 0.0), axis=2, keepdims=True)
        odd = jnp.sum(jnp.where(lo, 0.0, x), axis=2, keepdims=True)
        return even, odd

    def parity_rows(x):
        return jnp.concatenate([jnp.where(lo_rows, x, 0.0), jnp.where(lo_rows, 0.0, x)], axis=0)

    vk, rm = [], []
    for bi, p in pairs:
        sl = slice(p * pw, (p + 1) * pw)
        kp = parity_rows(k_ref[bi, :, sl])
        kb = jnp.concatenate([jnp.where(step_of_row == t, kp, 0.0) for t in range(tc)], axis=1)
        v8 = v_ref[bi, :, sl]
        va = jnp.concatenate([v8[:, 0:NC], v8[:, NC:pw]], axis=0)
        vk.append(_dot3(va, kb, _tn_dot, 0))
        rm.append(parity_rows(r_ref[bi, :, sl]))

    state = s_ref[...].reshape(shape)
    acc = [jnp.zeros((2 * tc, NC), F32) for _ in pairs]
    for t in range(tc):
        sa_e, sa_o = half_sums(state * rows(kk_ref, t))
        sa = jnp.where(lo, sa_e, sa_o)
        vk_t = jnp.concatenate([x[:, t * pw:(t + 1) * pw][None] for x in vk], axis=0)
        state = state * rows(w_ref, t) - sa * rows(b_ref, t) + vk_t
        for q in range(g):
            acc[q] = acc[q] + _dot3(jnp.where(step_of_row == t, rm[q], 0.0), state[q], _nt_dot, 1)
    s_ref[...] = state.reshape(s_ref.shape)
    for q, (bi, p) in enumerate(pairs):
        o_ref[bi, :, p * pw:(p + 1) * pw] = jnp.concatenate([acc[q][0:tc], acc[q][tc:2 * tc]], axis=1)

    @pl.when(tb == pl.num_programs(1) - 1)
    def _():
        sout_ref[...] = s_ref[...]


def _rwkv_scan(r, w, k, kk, b, v, s0, batch, seq):
    tc = SCAN_TC
    assert seq % tc == 0 and batch % 2 == 0
    nt = seq // tc
    npair = H_C // 2
    rows3 = lambda x: x.reshape(batch, seq, C_W)
    s0p = s0.reshape(batch, npair, 2, NC, NC).transpose(0, 1, 3, 2, 4).reshape(batch, npair, NC, 2 * NC)
    row_spec = pl.BlockSpec((2, tc, C_W), lambda bb, tb: (bb, tb, 0))
    st_spec = pl.BlockSpec((2, npair, NC, 2 * NC), lambda bb, tb: (bb, 0, 0, 0))
    o, s_fin = pl.pallas_call(
        functools.partial(_rwkv_scan_kernel, tc=tc),
        grid=(batch // 2, nt),
        in_specs=[row_spec] * 6 + [st_spec],
        out_specs=[row_spec, st_spec],
        out_shape=[jax.ShapeDtypeStruct((batch, seq, C_W), F32),
                   jax.ShapeDtypeStruct((batch, npair, NC, 2 * NC), F32)],
        scratch_shapes=[pltpu.VMEM((2, npair, NC, 2 * NC), F32)],
        compiler_params=_cparams(("parallel", "arbitrary")),
        name="rwkv_scan",
    )(rows3(r), rows3(w), rows3(k), rows3(kk), rows3(b), rows3(v), s0p)
    s_new = s_fin.reshape(batch, npair, NC, 2, NC).transpose(0, 1, 3, 2, 4).reshape(batch, H_C, NC, NC)
    return o.reshape(batch * seq, C_W), s_new


def _rwkv_post_kernel(o_ref, bonus_ref, g_ref, lng_ref, lnb_ref, ones_ref, out_ref):
    def seg_mean(x):
        return jnp.dot(x, ones_ref[...], precision=HIGHEST, preferred_element_type=F32) * (1.0 / NC)

    o = o_ref[...]
    oc = o - seg_mean(o)
    on = oc * lax.rsqrt(seg_mean(oc * oc) + GN_EPS_C)
    out_ref[...] = (on * lng_ref[...] + lnb_ref[...] + bonus_ref[...]) * g_ref[...]


def _rwkv_post(o, bonus, g, ln_g, ln_b, ones_bd):
    m = o.shape[0]
    tm = _pick_tile(m, 640)
    tok = pl.BlockSpec((tm, C_W), lambda i: (i, 0))
    vec = pl.BlockSpec((1, C_W), lambda i: (0, 0))
    return pl.pallas_call(
        _rwkv_post_kernel,
        grid=(m // tm,),
        in_specs=[tok, tok, tok, vec, vec, pl.BlockSpec((C_W, C_W), lambda i: (0, 0))],
        out_specs=tok,
        out_shape=jax.ShapeDtypeStruct((m, C_W), F32),
        compiler_params=_cparams(("parallel",)),
        name="rwkv_post",
    )(o, bonus, g, ln_g.reshape(1, C_W), ln_b.reshape(1, C_W), ones_bd)


def _perm_odd_columns():
    offs = np.cumsum([0, C_W, W_LORA, C_W, C_W, A_LORA, G_LORA])
    seg = lambda i: np.arange(offs[i], offs[i + 1])
    pc_perm = np.concatenate([seg(0), seg(2), seg(3), seg(1), seg(4), seg(5)])
    return pc_perm


def kernel(x_prompt, x_sample, cache_a_k, cache_a_v, state_b, state_c_wkv, state_c_shift, cache_d_k, cache_d_v,
           cache_d_logf, page_table, rel_bias, w_in_even, lam_params, subln_g, w_out_even, w_in_odd, c_mu, c_w0,
           c_w2, c_a0, c_a2, c_g2, c_kk, c_ka, c_rk, c_ln_g, c_ln_b, d_bf, w_out_odd, w_router, b_router, w_gate,
           w_up, w_down, ln1_g, ln1_b, ln2_g, ln2_b):
    batch, seq, d = x_prompt.shape
    dbatch, tdec, _ = x_sample.shape
    n_pages = page_table.shape[1]
    n_pool = cache_a_k.shape[1]
    past = n_pages * PAGE
    bt = batch * seq
    ds = dbatch * tdec
    pt_flat = page_table.reshape(-1).astype(jnp.int32)
    gp = _pick_tile(n_pages, PAGES_PER_STEP, 1)

    x = jnp.concatenate([x_prompt.reshape(bt, d), x_sample.reshape(ds, d)], axis=0)
    outs = {}

    def moe_block(x_in, layer):
        gates = _router(x_in, w_router, b_router)
        wgu = jnp.concatenate([w_gate[layer], w_up[layer]], axis=-1).astype(BF16)
        return _moe_ln(x_in, gates, wgu, w_down[layer].astype(BF16), ln2_g[layer], ln2_b[layer])

    lam_init0 = 0.8 - 0.6 * math.exp(-0.3 * 0)
    h0 = _proj(x, w_in_even[0].astype(BF16))
    ka_off = H_A * DKA
    va_off = 2 * H_A * DKA
    ka_all = h0[:, ka_off:ka_off + H_A * DKA]
    va_all = h0[:, va_off:va_off + H_A * DVA]
    outs["a_k_p"] = ka_all[:bt].reshape(1, batch, seq, H_A, DKA)
    outs["a_v_p"] = va_all[:bt].reshape(1, batch, seq, H_A, DVA)
    outs["a_k_s"] = ka_all[bt:].reshape(1, dbatch, tdec, H_A, DKA)
    outs["a_v_s"] = va_all[bt:].reshape(1, dbatch, tdec, H_A, DVA)

    oa_p = _diff_prompt(h0, batch, seq, rel_bias, lam_params[0], subln_g[0], lam_init0)
    qa_s = h0[bt:, 0:H_A * DKA].reshape(dbatch, tdec, H_A, DKA) * (DA ** -0.5)
    q_rows_a = _query_rows(qa_s, 2).astype(BF16)
    tab_d, tself_d = _diff_decode_tables(rel_bias, n_pages, gp, tdec)
    oa_s = _paged_attention(
        "diff", q_rows_a, cache_a_k.reshape(n_pool, PAGE * H_A, DKA), cache_a_v.reshape(n_pool, PAGE * H_A, DVA),
        pt_flat, n_pages, ka_all[bt:].reshape(dbatch, tdec, H_A * DKA), va_all[bt:].reshape(dbatch, tdec, H_A * DVA),
        (tab_d, tself_d, lam_params[0], subln_g[0].reshape(1, DVA)), 1.0, lam_init0)
    ob_p, sb_p = _retention(h0, 0, batch, seq, 0, jnp.zeros((batch, H_B * DKB, H_B * DVB), F32))
    ob_s, sb_s = _retention(h0, bt, dbatch, tdec, past, _state_to_bd(state_b[0].astype(F32)))
    outs["b_p"] = _bd_to_state(sb_p)[None]
    outs["b_s"] = _bd_to_state(sb_s)[None]
    mix_l = jnp.concatenate([oa_p, oa_s.reshape(ds, H_A * DVA)], axis=0)
    mix_r = jnp.concatenate([ob_p, ob_s], axis=0)
    x = _outproj_ln(mix_l, mix_r, x, w_out_even[0].astype(BF16), ln1_g[0], ln1_b[0])
    x = moe_block(x, 0)

    pc_perm = _perm_odd_columns()
    w_odd = w_in_odd[0]
    w_odd_p = jnp.concatenate([w_odd[:, pc_perm], w_odd[:, C_PROJ:],
                               jnp.zeros((d, ODD_IN_PAD - w_odd.shape[1]), w_odd.dtype)], axis=1).astype(BF16)
    h1 = _proj(x, w_odd_p)
    q_off = C_PROJ
    k_off = C_PROJ + D_W
    v_off = C_PROJ + 2 * D_W
    kd_all = h1[:, k_off:k_off + D_W]
    vd_all = h1[:, v_off:v_off + D_W]
    outs["d_k_p"] = kd_all[:bt].reshape(1, batch, seq, H_D, DD)
    outs["d_v_p"] = vd_all[:bt].reshape(1, batch, seq, H_D, DD)
    outs["d_k_s"] = kd_all[bt:].reshape(1, dbatch, tdec, H_D, DD)
    outs["d_v_s"] = vd_all[bt:].reshape(1, dbatch, tdec, H_D, DD)

    bf_pad = jnp.zeros((1, LANES), F32).at[0, :H_D].set(d_bf[0].astype(F32))
    lf_p, c_p = _fox_prep(h1, bf_pad, 0, batch, seq)
    lf_s, c_s = _fox_prep(h1, bf_pad, bt, dbatch, tdec)
    outs["d_lf_p"] = lf_p[:, :H_D].reshape(1, batch, seq, H_D)
    outs["d_lf_s"] = lf_s[:, :H_D].reshape(1, dbatch, tdec, H_D)
    od_p = _fox_prompt(h1, c_p, batch, seq)
    cn = c_s[:, :H_D].reshape(dbatch, tdec, H_D)
    cnt = cn.transpose(0, 2, 1)
    rho = cnt.reshape(dbatch, H_D * tdec, 1)
    logf_pool_t = cache_d_logf[0].astype(F32).transpose(0, 2, 1)
    kap = _suffix_sums(logf_pool_t, pt_flat, dbatch, n_pages)
    qd_s = h1[bt:, q_off:q_off + D_W].reshape(dbatch, tdec, H_D, DD)
    q_rows_d = _query_rows(qd_s, 1).astype(BF16)
    od_s = _paged_attention(
        "fox", q_rows_d, cache_d_k.reshape(n_pool, PAGE * H_D, DD), cache_d_v.reshape(n_pool, PAGE * H_D, DD),
        pt_flat, n_pages, kd_all[bt:].reshape(dbatch, tdec, D_W), vd_all[bt:].reshape(dbatch, tdec, D_W),
        (kap, rho, cnt), DD ** -0.5)

    pc_p = h1[:bt, :C_PROJ].reshape(batch, seq, C_PROJ)
    pc_s = h1[bt:, :C_PROJ].reshape(dbatch, tdec, C_PROJ)
    shift_p = jnp.zeros((batch, 1, C_PROJ), F32)
    shift_s = state_c_shift[0][:, pc_perm][:, None, :].astype(F32)
    prev = jnp.concatenate([
        jnp.concatenate([shift_p, pc_p[:, :-1]], axis=1).reshape(bt, C_PROJ),
        jnp.concatenate([shift_s, pc_s[:, :-1]], axis=1).reshape(ds, C_PROJ)], axis=0)
    inv_perm = np.argsort(pc_perm)
    outs["c_shift_p"] = pc_p[:, -1][:, inv_perm][None]
    outs["c_shift_s"] = pc_s[:, -1][:, inv_perm][None]
    w2p = jnp.concatenate([c_w2[0], jnp.zeros((A_LORA, C_W), F32)], axis=0).astype(BF16)
    a2p = jnp.concatenate([jnp.zeros((W_LORA, C_W), F32), c_a2[0]], axis=0).astype(BF16)
    ones_bd = jnp.asarray(np.kron(np.eye(H_C, dtype=np.float32), np.ones((NC, NC), np.float32)))
    r_, w_, k_, v_, kk_, b_, g_, bonus = _rwkv_prep(
        h1, prev, c_mu[0][pc_perm], c_w0[0], c_a0[0], c_kk[0], c_ka[0], c_rk[0].reshape(-1),
        w2p, a2p, c_g2[0].astype(BF16), ones_bd)
    o_p, sc_p = _rwkv_scan(r_[:bt], w_[:bt], k_[:bt], kk_[:bt], b_[:bt], v_[:bt],
                           jnp.zeros((batch, H_C, NC, NC), F32), batch, seq)
    o_s, sc_s = _rwkv_scan(r_[bt:], w_[bt:], k_[bt:], kk_[bt:], b_[bt:], v_[bt:],
                           state_c_wkv[0].astype(F32), dbatch, tdec)
    outs["c_wkv_p"] = sc_p[None]
    outs["c_wkv_s"] = sc_s[None]
    oc = _rwkv_post(jnp.concatenate([o_p, o_s], axis=0), bonus, g_, c_ln_g[0], c_ln_b[0], ones_bd)
    mix_r = jnp.concatenate([od_p, od_s.reshape(ds, D_W)], axis=0)
    x = _outproj_ln(oc, mix_r, x, w_out_odd[0].astype(BF16), ln1_g[1], ln1_b[1])
    x = moe_block(x, 1)

    y_prompt = x[:bt].reshape(batch, seq, d)
    y_sample = x[bt:].reshape(dbatch, tdec, d)
    return (y_prompt, y_sample,
            outs["a_k_p"], outs["a_v_p"], outs["a_k_s"], outs["a_v_s"],
            outs["b_p"], outs["b_s"],
            outs["c_wkv_p"], outs["c_wkv_s"], outs["c_shift_p"], outs["c_shift_s"],
            outs["d_k_p"], outs["d_v_p"], outs["d_lf_p"], outs["d_k_s"], outs["d_v_s"], outs["d_lf_s"])
```

```python
import functools
import math

import numpy as np
import jax
import jax.numpy as jnp
from jax import lax
from jax.experimental import pallas as pl
from jax.experimental.pallas import tpu as pltpu

F32 = jnp.float32
BF16 = jnp.bfloat16
HIGHEST = lax.Precision.HIGHEST

D_MODEL = 1024
DEPTH = 2
PAGE = 128
MIX_W = D_MODEL // 2
DA = 64
DKA = 2 * DA
DVA = 2 * DA
H_A = MIX_W // DVA
DKB = 64
DVB = 128
H_B = MIX_W // DVB
RET_CHUNK = 128
NC = 64
H_C = MIX_W // NC
W_LORA = 64
A_LORA = 64
G_LORA = 128
GN_EPS_C = 64e-5
DD = 128
H_D = MIX_W // DD
N_BUCKETS = 32
MAX_DISTANCE = 128
N_EXPERTS = 16
N_GROUPS = 4
EPG = N_EXPERTS // N_GROUPS
D_EXPERT = D_MODEL // 4
DN_ALPHA = (2 * DEPTH) ** 0.25
LN_EPS = 1e-5
RMS_EPS = 1e-5
C_W = H_C * NC
C_PROJ = C_W + W_LORA + C_W + C_W + A_LORA + G_LORA
D_W = H_D * DD
EVEN_IN = 2 * H_A * DKA + H_A * DVA + 2 * H_B * DKB + 2 * H_B * DVB
ODD_IN_PAD = C_PROJ + 3 * D_W + 128

LANES = 128
SUBLANES = 8
VMEM_LIMIT = 56 * 1024 * 1024
NEG = -1e30
ATT_TQ = 256
ATT_FAR_TILES = 8
FOX_TQ = 256
FOX_FAR_TILES = 8
PAGES_PER_STEP = 16
SUFFIX_PAGES_PER_STEP = 32
SCAN_TC = 8


def _cparams(sem):
    return pltpu.CompilerParams(dimension_semantics=sem, vmem_limit_bytes=VMEM_LIMIT)


def _pick_tile(m, cap, mult=8):
    best = None
    for t in range(mult, min(m, cap) + 1, mult):
        if m % t == 0:
            best = t
    assert best is not None, (m, cap, mult)
    return best


def _layer_norm_rows(z, g, b):
    mu = jnp.mean(z, axis=-1, keepdims=True)
    zc = z - mu
    var = jnp.mean(zc * zc, axis=-1, keepdims=True)
    return zc * lax.rsqrt(var + LN_EPS) * g + b


def _sigmoid(x):
    return 1.0 / (1.0 + jnp.exp(-x))


def _softplus(y):
    return jnp.maximum(y, 0.0) + jnp.log1p(jnp.exp(-jnp.abs(y)))


def _proj_kernel(x_ref, w_ref, o_ref):
    o_ref[...] = jnp.dot(x_ref[...].astype(BF16), w_ref[...], preferred_element_type=F32)


def _proj(x, w_bf):
    m, k = x.shape
    n = w_bf.shape[1]
    tm = _pick_tile(m, 320)
    return pl.pallas_call(
        _proj_kernel,
        grid=(m // tm,),
        in_specs=[pl.BlockSpec((tm, k), lambda i: (i, 0)),
                  pl.BlockSpec((k, n), lambda i: (0, 0))],
        out_specs=pl.BlockSpec((tm, n), lambda i: (i, 0)),
        out_shape=jax.ShapeDtypeStruct((m, n), F32),
        compiler_params=_cparams(("parallel",)),
        name="proj",
    )(x, w_bf)


def _outproj_kernel(a_ref, b_ref, x_ref, w_ref, g_ref, be_ref, o_ref):
    half = a_ref.shape[1]
    y = jnp.dot(a_ref[...].astype(BF16), w_ref[0:half, :], preferred_element_type=F32)
    y = y + jnp.dot(b_ref[...].astype(BF16), w_ref[half:2 * half, :], preferred_element_type=F32)
    z = DN_ALPHA * x_ref[...] + y
    o_ref[...] = _layer_norm_rows(z, g_ref[...], be_ref[...])


def _outproj_ln(mix_l, mix_r, x, w_bf, g, b):
    m, d = x.shape
    half = mix_l.shape[1]
    tm = _pick_tile(m, 640)
    return pl.pallas_call(
        _outproj_kernel,
        grid=(m // tm,),
        in_specs=[pl.BlockSpec((tm, half), lambda i: (i, 0)),
                  pl.BlockSpec((tm, half), lambda i: (i, 0)),
                  pl.BlockSpec((tm, d), lambda i: (i, 0)),
                  pl.BlockSpec((2 * half, d), lambda i: (0, 0)),
                  pl.BlockSpec((1, d), lambda i: (0, 0)),
                  pl.BlockSpec((1, d), lambda i: (0, 0))],
        out_specs=pl.BlockSpec((tm, d), lambda i: (i, 0)),
        out_shape=jax.ShapeDtypeStruct((m, d), F32),
        compiler_params=_cparams(("parallel",)),
        name="outproj_ln",
    )(mix_l, mix_r, x, w_bf, g.reshape(1, d), b.reshape(1, d))


def _router_kernel(x_ref, wr_ref, br_ref, g_ref):
    logits = _nt_dot(wr_ref[...], x_ref[...].astype(BF16))
    mx = jnp.max(logits, axis=0, keepdims=True)
    ex = jnp.exp(logits - mx)
    scores = ex / jnp.sum(ex, axis=0, keepdims=True)
    sel = scores + br_ref[...]
    rows = [sel[e:e + 1, :] for e in range(N_EXPERTS)]
    srow = [scores[e:e + 1, :] for e in range(N_EXPERTS)]
    grp = []
    for gi in range(N_GROUPS):
        a, b, c, d = rows[EPG * gi:EPG * gi + EPG]
        hi1, lo1 = jnp.maximum(a, b), jnp.minimum(a, b)
        hi2, lo2 = jnp.maximum(c, d), jnp.minimum(c, d)
        top1 = jnp.maximum(hi1, hi2)
        top2 = jnp.maximum(jnp.minimum(hi1, hi2), jnp.maximum(lo1, lo2))
        grp.append(top1 + top2)
    best_v = grp[0]
    best_i = jnp.zeros_like(best_v, dtype=jnp.int32)
    for gi in range(1, N_GROUPS):
        better = grp[gi] > best_v
        best_v = jnp.where(better, grp[gi], best_v)
        best_i = jnp.where(better, gi, best_i)
    chosen = []
    for e in range(N_EXPERTS):
        gi = e // EPG
        rank = jnp.zeros_like(best_i)
        for j in range(EPG * gi, EPG * gi + EPG):
            if j == e:
                continue
            beats = (rows[j] >= rows[e]) if j < e else (rows[j] > rows[e])
            rank = rank + beats.astype(jnp.int32)
        chosen.append((best_i == gi) & (rank < 2))
    wsum = jnp.zeros_like(best_v)
    for e in range(N_EXPERTS):
        wsum = wsum + jnp.where(chosen[e], srow[e], 0.0)
    for e in range(N_EXPERTS):
        g_ref[e:e + 1, :] = jnp.where(chosen[e], srow[e] / wsum, 0.0)


def _router(x, w_router, b_router):
    m, d = x.shape
    tm = _pick_tile(m, 1280, LANES)
    gt = pl.pallas_call(
        _router_kernel,
        grid=(m // tm,),
        in_specs=[pl.BlockSpec((tm, d), lambda i: (i, 0)),
                  pl.BlockSpec((N_EXPERTS, d), lambda i: (0, 0)),
                  pl.BlockSpec((N_EXPERTS, 1), lambda i: (0, 0))],
        out_specs=pl.BlockSpec((N_EXPERTS, tm), lambda i: (0, i)),
        out_shape=jax.ShapeDtypeStruct((N_EXPERTS, m), F32),
        compiler_params=_cparams(("parallel",)),
        name="router",
    )(x, w_router.T.astype(BF16), b_router.reshape(N_EXPERTS, 1))
    return gt.T


def _moe_kernel(x_ref, gate_ref, wgu_ref, wd_ref, g_ref, b_ref, o_ref, xb_ref, acc_ref):
    e = pl.program_id(1)

    @pl.when(e == 0)
    def _():
        xb_ref[...] = x_ref[...].astype(BF16)
        acc_ref[...] = jnp.zeros_like(acc_ref)

    h = jnp.dot(xb_ref[...], wgu_ref[0], preferred_element_type=F32)
    gates = gate_ref[...]
    lane = lax.broadcasted_iota(jnp.int32, gates.shape, 1)
    gcol = jnp.sum(jnp.where(lane == e, gates, 0.0), axis=1, keepdims=True)
    hg = h[:, :D_EXPERT]
    hu = h[:, D_EXPERT:]
    act = hg * _sigmoid(hg) * hu * gcol
    acc_ref[...] += jnp.dot(act.astype(BF16), wd_ref[0], preferred_element_type=F32)

    @pl.when(e == N_EXPERTS - 1)
    def _():
        z = DN_ALPHA * x_ref[...] + acc_ref[...]
        o_ref[...] = _layer_norm_rows(z, g_ref[...], b_ref[...])


def _moe_ln(x, gates, wgu_bf, wd_bf, g, b):
    m, d = x.shape
    tm = _pick_tile(m, 640)
    return pl.pallas_call(
        _moe_kernel,
        grid=(m // tm, N_EXPERTS),
        in_specs=[pl.BlockSpec((tm, d), lambda i, e: (i, 0)),
                  pl.BlockSpec((tm, N_EXPERTS), lambda i, e: (i, 0)),
                  pl.BlockSpec((1, d, 2 * D_EXPERT), lambda i, e: (e, 0, 0)),
                  pl.BlockSpec((1, D_EXPERT, d), lambda i, e: (e, 0, 0)),
                  pl.BlockSpec((1, d), lambda i, e: (0, 0)),
                  pl.BlockSpec((1, d), lambda i, e: (0, 0))],
        out_specs=pl.BlockSpec((tm, d), lambda i, e: (i, 0)),
        out_shape=jax.ShapeDtypeStruct((m, d), F32),
        scratch_shapes=[pltpu.VMEM((tm, d), BF16), pltpu.VMEM((tm, d), F32)],
        compiler_params=_cparams(("parallel", "arbitrary")),
        name="moe_ln",
    )(x, gates, wgu_bf, wd_bf, g.reshape(1, d), b.reshape(1, d))


def _softmax_update(s, v_bf, m_ref, l_ref, acc_ref, idx):
    m_prev = m_ref[idx][:, 0:1]
    l_prev = l_ref[idx][:, 0:1]
    m_new = jnp.maximum(m_prev, jnp.max(s, axis=1, keepdims=True))
    alpha = jnp.exp(m_prev - m_new)
    p = jnp.exp(s - m_new)
    l_new = alpha * l_prev + jnp.sum(p, axis=1, keepdims=True)
    acc_ref[idx] = alpha * acc_ref[idx] + jnp.dot(p.astype(BF16), v_bf, preferred_element_type=F32)
    m_ref[idx] = jnp.broadcast_to(m_new, m_ref.shape[1:])
    l_ref[idx] = jnp.broadcast_to(l_new, l_ref.shape[1:])


def _softmax_init(m_ref, l_ref, acc_ref):
    m_ref[...] = jnp.full(m_ref.shape, NEG, F32)
    l_ref[...] = jnp.zeros(l_ref.shape, F32)
    acc_ref[...] = jnp.zeros(acc_ref.shape, F32)


def _far_loops(nfar, body, far_tiles):
    nwide = nfar // far_tiles
    npair = nwide // 2

    def wide_pair(j, carry):
        body(2 * j, far_tiles)
        body(2 * j + 1, far_tiles)
        return carry

    def wide(j, carry):
        body(j, far_tiles)
        return carry

    def single(j, carry):
        body(j, 1)
        return carry

    lax.fori_loop(0, npair, wide_pair, 0)
    lax.fori_loop(2 * npair, nwide, wide, 0)
    done = nwide * far_tiles
    half = far_tiles // 2
    if half > 1:
        take_half = (nfar - done) >= half

        @pl.when(take_half)
        def _():
            body(done // half, half)

        done = done + jnp.where(take_half, half, 0)
    lax.fori_loop(done, nfar, single, 0)


def _nt_dot(a, b):
    return lax.dot_general(a, b, (((1,), (1,)), ((), ())), preferred_element_type=F32)


def _tn_dot(a, b):
    return lax.dot_general(a, b, (((0,), (0,)), ((), ())), preferred_element_type=F32)


def _t5_bucket_np(dist):
    max_exact = N_BUCKETS // 2
    d = np.maximum(dist, 0)
    ratio = (np.maximum(d, 1).astype(np.float32) / np.float32(max_exact)).astype(np.float32)
    log_ratio = (np.log(ratio).astype(np.float32) / np.float32(math.log(MAX_DISTANCE / max_exact))).astype(np.float32)
    large = np.minimum(max_exact + (log_ratio * np.float32(N_BUCKETS - max_exact)).astype(np.int32), N_BUCKETS - 1)
    return np.where(d < max_exact, d, large).astype(np.int32)


_T5_FAR = int(np.min(np.nonzero(_t5_bucket_np(np.arange(4 * MAX_DISTANCE)) == N_BUCKETS - 1)[0]))
assert np.all(_t5_bucket_np(np.arange(_T5_FAR, 8 * MAX_DISTANCE)) == N_BUCKETS - 1)


def _lam_from_params(lp):
    s01 = jnp.sum(lp[0:1, :] * lp[1:2, :], axis=1, keepdims=True)
    s23 = jnp.sum(lp[2:3, :] * lp[3:4, :], axis=1, keepdims=True)
    return jnp.exp(s01) - jnp.exp(s23)


def _diff_finalize(o0, o1, lam, gain, lam_init):
    o = o0 - lam * o1
    ms = jnp.mean(o * o, axis=-1, keepdims=True)
    return o * lax.rsqrt(ms + RMS_EPS) * gain * (1.0 - lam_init)


def _diff_prompt_kernel(q_ref, k_ref, v_ref, tab_ref, lp_ref, gain_ref, o_ref,
                        kb_ref, vb_ref, m_ref, l_ref, acc_ref, *, tq, lam_init):
    i = pl.program_id(2)

    @pl.when(i == 0)
    def _():
        kb_ref[...] = k_ref[...].astype(BF16)
        vb_ref[...] = v_ref[...].astype(BF16)

    _softmax_init(m_ref, l_ref, acc_ref)
    qf = q_ref[...] * (DA ** -0.5)
    lane = lax.broadcasted_iota(jnp.int32, qf.shape, 1)
    qm = (jnp.where(lane < DA, qf, 0.0).astype(BF16), jnp.where(lane >= DA, qf, 0.0).astype(BF16))

    def chunk_at(ks, width, bias):
        kc = kb_ref[pl.ds(ks, width), :]
        vc = vb_ref[pl.ds(ks, width), :]
        for mi in range(2):
            s = _nt_dot(qm[mi], kc)
            if bias is not None:
                s = s + bias
            _softmax_update(s, vc, m_ref, l_ref, acc_ref, mi)

    def far_chunk(j, tiles):
        width = tiles * tq
        chunk_at(pl.multiple_of(j * width, width), width, None)

    _far_loops(jnp.maximum(i - 1, 0), far_chunk, ATT_FAR_TILES)

    @pl.when(i >= 1)
    def _():
        chunk_at(pl.multiple_of((i - 1) * tq, tq), 2 * tq, jnp.concatenate([tab_ref[0, 1], tab_ref[0, 0]], axis=1))

    @pl.when(i == 0)
    def _():
        chunk_at(0, tq, tab_ref[0, 0])
    lam = _lam_from_params(lp_ref[...]) + lam_init
    o0 = acc_ref[0] / l_ref[0][:, 0:1]
    o1 = acc_ref[1] / l_ref[1][:, 0:1]
    o_ref[...] = _diff_finalize(o0, o1, lam, gain_ref[...], lam_init)


def _diff_tables(rel_bias, tq):
    qi = np.arange(tq)[:, None]
    kj = np.arange(tq)[None, :]
    rb = rel_bias.astype(F32)
    rel = rb - rb[N_BUCKETS - 1][None, :]
    def lookup(bucket):
        onehot = jnp.asarray(bucket)[..., None] == jnp.arange(N_BUCKETS)
        return jnp.sum(jnp.where(onehot[..., None], rel[None, None], 0.0), axis=2)

    diag = jnp.where(jnp.asarray(qi >= kj)[..., None], lookup(_t5_bucket_np(qi - kj)), NEG)
    prev = lookup(_t5_bucket_np(tq + qi - kj))
    return jnp.moveaxis(jnp.stack([diag, prev], 0), -1, 0)


def _diff_prompt(h0, batch, seq, rel_bias, lam_params, subln_g, lam_init):
    tq = min(ATT_TQ, seq)
    assert seq % tq == 0 and tq + 1 >= _T5_FAR
    nq = seq // tq
    tab = _diff_tables(rel_bias, tq)
    kern = functools.partial(_diff_prompt_kernel, tq=tq, lam_init=lam_init)
    qoff, koff, voff = 0, H_A, 2 * H_A
    return pl.pallas_call(
        kern,
        grid=(batch, H_A, nq),
        in_specs=[pl.BlockSpec((tq, DKA), lambda b, h, i: (b * nq + i, qoff + h)),
                  pl.BlockSpec((seq, DKA), lambda b, h, i: (b, koff + h)),
                  pl.BlockSpec((seq, DVA), lambda b, h, i: (b, voff + h)),
                  pl.BlockSpec((1, 2, tq, tq), lambda b, h, i: (h, 0, 0, 0)),
                  pl.BlockSpec((4, DA), lambda b, h, i: (0, 0)),
                  pl.BlockSpec((1, DVA), lambda b, h, i: (0, 0))],
        out_specs=pl.BlockSpec((tq, DVA), lambda b, h, i: (b * nq + i, h)),
        out_shape=jax.ShapeDtypeStruct((batch * seq, H_A * DVA), F32),
        scratch_shapes=[pltpu.VMEM((seq, DKA), BF16), pltpu.VMEM((seq, DVA), BF16),
                        pltpu.VMEM((2, tq, LANES), F32), pltpu.VMEM((2, tq, LANES), F32),
                        pltpu.VMEM((2, tq, DVA), F32)],
        compiler_params=_cparams(("parallel", "parallel", "arbitrary")),
        name="diff_prompt",
    )(h0, h0, h0, tab, lam_params, subln_g.reshape(1, DVA))


def _fox_prompt_kernel(q_ref, k_ref, v_ref, c_ref, kap_ref, o_ref,
                       kb_ref, vb_ref, m_ref, l_ref, acc_ref, *, tq):
    h = pl.program_id(1)
    i = pl.program_id(2)

    @pl.when(i == 0)
    def _():
        kb_ref[...] = k_ref[...].astype(BF16)
        vb_ref[...] = v_ref[...].astype(BF16)

    _softmax_init(m_ref, l_ref, acc_ref)
    qb = q_ref[...].astype(BF16)
    cblk = c_ref[...]
    lane = lax.broadcasted_iota(jnp.int32, cblk.shape, 1)
    rho = jnp.sum(jnp.where(lane == h, cblk, 0.0), axis=1, keepdims=True)

    def chunk(j, masked, tiles=1):
        width = tiles * tq
        ks = pl.multiple_of(j * width, width)
        kc = kb_ref[pl.ds(ks, width), :]
        vc = vb_ref[pl.ds(ks, width), :]
        s = _nt_dot(qb, kc) * (DD ** -0.5) + (rho - kap_ref[0, :, pl.ds(ks, width)])
        if masked:
            r = lax.broadcasted_iota(jnp.int32, s.shape, 0)
            c = lax.broadcasted_iota(jnp.int32, s.shape, 1)
            s = jnp.where(r >= c, s, NEG)
        _softmax_update(s, vc, m_ref, l_ref, acc_ref, 0)

    _far_loops(i, lambda j, tiles: chunk(j, False, tiles), FOX_FAR_TILES)
    chunk(i, True)
    o_ref[...] = acc_ref[0] / l_ref[0][:, 0:1]


def _fox_prompt(h1, c_all, batch, seq):
    tq = min(FOX_TQ, seq)
    assert seq % tq == 0
    nq = seq // tq
    qoff = C_PROJ // LANES
    koff = qoff + H_D
    voff = koff + H_D
    kap = c_all[:batch * seq, :H_D].reshape(batch, seq, H_D).transpose(0, 2, 1).reshape(batch * H_D, 1, seq)
    kern = functools.partial(_fox_prompt_kernel, tq=tq)
    return pl.pallas_call(
        kern,
        grid=(batch, H_D, nq),
        in_specs=[pl.BlockSpec((tq, DD), lambda b, h, i: (b * nq + i, qoff + h)),
                  pl.BlockSpec((seq, DD), lambda b, h, i: (b, koff + h)),
                  pl.BlockSpec((seq, DD), lambda b, h, i: (b, voff + h)),
                  pl.BlockSpec((tq, LANES), lambda b, h, i: (b * nq + i, 0)),
                  pl.BlockSpec((1, 1, seq), lambda b, h, i: (b * H_D + h, 0, 0))],
        out_specs=pl.BlockSpec((tq, DD), lambda b, h, i: (b * nq + i, h)),
        out_shape=jax.ShapeDtypeStruct((batch * seq, D_W), F32),
        scratch_shapes=[pltpu.VMEM((seq, DD), BF16), pltpu.VMEM((seq, DD), BF16),
                        pltpu.VMEM((1, tq, LANES), F32), pltpu.VMEM((1, tq, LANES), F32),
                        pltpu.VMEM((1, tq, DD), F32)],
        compiler_params=_cparams(("parallel", "parallel", "arbitrary")),
        name="fox_prompt",
    )(h1, h1, h1, c_all, kap)


def _fox_prep_kernel(fd_ref, bf_ref, tri_ref, lf_ref, c_ref, carry_ref):
    @pl.when(pl.program_id(1) == 0)
    def _():
        carry_ref[...] = jnp.zeros_like(carry_ref)

    x = fd_ref[...] + bf_ref[...]
    lf = -_softplus(-x)
    inc = jnp.dot(tri_ref[...], lf, precision=HIGHEST, preferred_element_type=F32) + carry_ref[0:1, :]
    lf_ref[...] = lf
    c_ref[...] = inc
    tm = inc.shape[0]
    carry_ref[...] = jnp.broadcast_to(inc[tm - 1:tm, :], carry_ref.shape)


def _fox_prep(h1, bf_pad, row0, batch, seq):
    tm = _pick_tile(seq, 256)
    nt = seq // tm
    assert row0 % tm == 0
    blk0 = row0 // tm
    col = (ODD_IN_PAD - LANES) // LANES
    tri = jnp.asarray(np.tril(np.ones((tm, tm), np.float32)))
    return pl.pallas_call(
        _fox_prep_kernel,
        grid=(batch, nt),
        in_specs=[pl.BlockSpec((tm, LANES), lambda b, i: (blk0 + b * nt + i, col)),
                  pl.BlockSpec((1, LANES), lambda b, i: (0, 0)),
                  pl.BlockSpec((tm, tm), lambda b, i: (0, 0))],
        out_specs=[pl.BlockSpec((tm, LANES), lambda b, i: (b * nt + i, 0)),
                   pl.BlockSpec((tm, LANES), lambda b, i: (b * nt + i, 0))],
        out_shape=[jax.ShapeDtypeStruct((batch * seq, LANES), F32),
                   jax.ShapeDtypeStruct((batch * seq, LANES), F32)],
        scratch_shapes=[pltpu.VMEM((8, LANES), F32)],
        compiler_params=_cparams(("parallel", "arbitrary")),
        name="fox_prep",
    )(h1, bf_pad, tri)


def _suffix_kernel(pt_ref, *refs, gpages, nheads):
    lf_refs = refs[:gpages]
    ut_ref, later_ref, o_ref, carry_ref = refs[gpages:]
    hp = carry_ref.shape[0]

    @pl.when(pl.program_id(1) == 0)
    def _():
        carry_ref[...] = jnp.zeros_like(carry_ref)

    lf = jnp.concatenate([r[...] for r in lf_refs], axis=0)
    incl = jnp.dot(lf, ut_ref[...], precision=HIGHEST, preferred_element_type=F32)
    totals = jnp.broadcast_to(incl[:, 0:1], incl.shape)
    later = jnp.dot(later_ref[...], totals, precision=HIGHEST, preferred_element_type=F32)
    carry = carry_ref[...]
    out = incl - lf + later + jnp.concatenate([carry] * gpages, axis=0)
    for j in range(gpages):
        o_ref[0, :, j * PAGE:(j + 1) * PAGE] = out[j * hp:j * hp + nheads, :]
    carry_ref[...] = carry + later[0:hp] + totals[0:hp]


def _suffix_sums(logf_pool_t, page_table_flat, dbatch, n_pages):
    gp = _pick_tile(n_pages, SUFFIX_PAGES_PER_STEP, 1)
    ng = n_pages // gp
    nheads = logf_pool_t.shape[1]
    hp = -(-nheads // SUBLANES) * SUBLANES
    pool = jnp.pad(logf_pool_t, ((0, 0), (0, hp - nheads), (0, 0)))
    ut = jnp.asarray(np.tril(np.ones((PAGE, PAGE), np.float32)))
    later = jnp.asarray(np.kron(np.triu(np.ones((gp, gp), np.float32), 1), np.eye(hp, dtype=np.float32)))

    def page_map(j):
        return lambda b, g, pt: (pt[b * n_pages + (ng - 1 - g) * gp + j], 0, 0)

    in_specs = [pl.BlockSpec((None, hp, PAGE), page_map(j)) for j in range(gp)]
    in_specs.append(pl.BlockSpec((PAGE, PAGE), lambda b, g, pt: (0, 0)))
    in_specs.append(pl.BlockSpec((gp * hp, gp * hp), lambda b, g, pt: (0, 0)))
    return pl.pallas_call(
        functools.partial(_suffix_kernel, gpages=gp, nheads=nheads),
        grid_spec=pltpu.PrefetchScalarGridSpec(
            num_scalar_prefetch=1,
            grid=(dbatch, ng),
            in_specs=in_specs,
            out_specs=pl.BlockSpec((1, nheads, gp * PAGE), lambda b, g, pt: (b, 0, ng - 1 - g)),
            scratch_shapes=[pltpu.VMEM((hp, LANES), F32)]),
        out_shape=jax.ShapeDtypeStruct((dbatch, nheads, n_pages * PAGE), F32),
        compiler_params=_cparams(("parallel", "arbitrary")),
        name="fox_suffix",
    )(page_table_flat, *([pool] * gp), ut, later)


def _paged_kernel(pt_ref, *refs, gpages, tdec, mode, scale, lam_init, chunk, nheads):
    q_ref = refs[0]
    k_refs = refs[1:1 + gpages]
    v_refs = refs[1 + gpages:1 + 2 * gpages]
    rest = refs[1 + 2 * gpages:]
    if mode == "diff":
        tab_ref, tself_ref, kn_ref, vn_ref, lp_ref, gain_ref, o_ref, s_ref, vb_ref = rest
    else:
        kap_ref, rho_ref, cnt_ref, kn_ref, vn_ref, o_ref, s_ref, vb_ref = rest
    g = pl.program_id(1)
    ng = pl.num_programs(1)
    gw = gpages * PAGE
    rows, past = s_ref.shape
    rph = rows // nheads
    dh = q_ref.shape[2]
    off = pl.multiple_of(g * gw, gw)

    def head_rows(ref, h):
        return ref[pl.ds(h, PAGE, stride=nheads), :]

    for h in range(nheads):
        hs = slice(h * rph, (h + 1) * rph)
        kh = jnp.concatenate([head_rows(r, h).astype(BF16) for r in k_refs], axis=0)
        s = _nt_dot(q_ref[0, hs, :], kh)
        if scale != 1.0:
            s = s * scale
        if mode == "diff":
            s = s + tab_ref[0, hs, :]
        else:
            s = (s + rho_ref[0, hs, :]) + kap_ref[0, h:h + 1, :]
        s_ref[hs, pl.ds(off, gw)] = s
        for j in range(gpages):
            vb_ref[h, pl.ds(off + j * PAGE, PAGE), :] = head_rows(v_refs[j], h).astype(BF16)

    @pl.when(g == ng - 1)
    def _():
        tails = []
        for h in range(nheads):
            hs = slice(h * rph, (h + 1) * rph)
            kn = kn_ref[0, :, h * dh:(h + 1) * dh].astype(BF16)
            t = _nt_dot(q_ref[0, hs, :], kn)
            if scale != 1.0:
                t = t * scale
            if mode == "diff":
                t = t + tself_ref[hs, :]
            else:
                t = (t + rho_ref[0, hs, :]) - cnt_ref[0, h:h + 1, :]
                r = lax.broadcasted_iota(jnp.int32, t.shape, 0)
                c = lax.broadcasted_iota(jnp.int32, t.shape, 1)
                t = jnp.where(c <= r, t, NEG)
            tails.append(t)
        tail = jnp.concatenate(tails, axis=0)
        if mode == "diff":
            lam = _lam_from_params(lp_ref[...]) + lam_init
        nchunk = past // chunk

        def sc(ci, hs=slice(None)):
            return s_ref[hs, pl.ds(pl.multiple_of(ci * chunk, chunk), chunk)]

        m = lax.fori_loop(0, nchunk, lambda ci, mm: jnp.maximum(mm, jnp.max(sc(ci), axis=1, keepdims=True)),
                          jnp.max(tail, axis=1, keepdims=True))
        den = lax.fori_loop(0, nchunk, lambda ci, dd: dd + jnp.sum(jnp.exp(sc(ci) - m), axis=1, keepdims=True),
                            jnp.sum(jnp.exp(tail - m), axis=1, keepdims=True))

        def weights(sv, h):
            hs = slice(h * rph, (h + 1) * rph)
            p = jnp.exp(sv - m[hs]) / den[hs]
            if mode == "diff":
                p = p[0:tdec] - lam * p[tdec:2 * tdec]
            return p.astype(BF16)

        def pv_body(ci, accs):
            cs = pl.multiple_of(ci * chunk, chunk)
            return tuple(
                accs[h] + jnp.dot(weights(sc(ci, slice(h * rph, (h + 1) * rph)), h), vb_ref[h, pl.ds(cs, chunk), :],
                                  preferred_element_type=F32)
                for h in range(nheads))

        init = tuple(
            jnp.dot(weights(tails[h], h), vn_ref[0, :, h * dh:(h + 1) * dh].astype(BF16), preferred_element_type=F32)
            for h in range(nheads))
        outs = lax.fori_loop(0, nchunk, pv_body, init)
        for h in range(nheads):
            oh = outs[h]
            if mode == "diff":
                ms = jnp.mean(oh * oh, axis=-1, keepdims=True)
                oh = oh * lax.rsqrt(ms + RMS_EPS) * gain_ref[...] * (1.0 - lam_init)
            o_ref[0, :, h * dh:(h + 1) * dh] = oh


def _query_rows(q, n_maps):
    db, t, nh, w = q.shape
    dm = w // n_maps
    qh = q.transpose(0, 2, 1, 3)
    sel = jnp.asarray(np.kron(np.eye(n_maps, dtype=np.float32), np.ones((1, dm), np.float32)))
    rows = qh[:, :, None, :, :] * sel[None, None, :, None, :]
    return rows.reshape(db, nh * n_maps * t, w)


def _paged_attention(mode, q_rows_bf, k_pool, v_pool, page_table_flat, n_pages, k_new, v_new, extra, scale,
                     lam_init=0.0):
    dbatch, rows, dh = q_rows_bf.shape
    tdec = k_new.shape[1]
    width = k_new.shape[2]
    nheads = width // dh
    gp = _pick_tile(n_pages, PAGES_PER_STEP, 1)
    ng = n_pages // gp
    past = n_pages * PAGE
    chunk = _pick_tile(past, 2048, LANES)
    pad_rows = lambda x: jnp.pad(x, ((0, 0), (0, PAGE - tdec), (0, 0)))
    k_new = pad_rows(k_new)
    v_new = pad_rows(v_new)

    def page_map(j):
        return lambda b, g, pt: (pt[b * n_pages + g * gp + j], 0, 0)

    in_specs = [pl.BlockSpec((1, rows, dh), lambda b, g, pt: (b, 0, 0))]
    in_specs += [pl.BlockSpec((None, PAGE * nheads, dh), page_map(j)) for j in range(gp)]
    in_specs += [pl.BlockSpec((None, PAGE * nheads, dh), page_map(j)) for j in range(gp)]
    new_spec = pl.BlockSpec((1, PAGE, width), lambda b, g, pt: (b, 0, 0))
    if mode == "diff":
        tab, tself, lam_params, gain = extra
        tself = jnp.pad(tself, ((0, 0), (0, PAGE - tdec)), constant_values=NEG)
        in_specs += [pl.BlockSpec((1, rows, gp * PAGE), lambda b, g, pt: ((g + 1) // ng, 0, 0)),
                     pl.BlockSpec((rows, PAGE), lambda b, g, pt: (0, 0)),
                     new_spec, new_spec,
                     pl.BlockSpec((4, DA), lambda b, g, pt: (0, 0)),
                     pl.BlockSpec((1, DVA), lambda b, g, pt: (0, 0))]
        operands = [tab, tself, k_new, v_new, lam_params, gain]
    else:
        kap, rho, cnt = extra
        cnt = jnp.pad(cnt, ((0, 0), (0, 0), (0, PAGE - tdec)))
        in_specs += [pl.BlockSpec((1, H_D, gp * PAGE), lambda b, g, pt: (b, 0, g)),
                     pl.BlockSpec((1, rows, 1), lambda b, g, pt: (b, 0, 0)),
                     pl.BlockSpec((1, H_D, PAGE), lambda b, g, pt: (b, 0, 0)),
                     new_spec, new_spec]
        operands = [kap, rho, cnt, k_new, v_new]
    kern = functools.partial(_paged_kernel, gpages=gp, tdec=tdec, mode=mode, scale=scale, lam_init=lam_init,
                             chunk=chunk, nheads=nheads)
    return pl.pallas_call(
        kern,
        grid_spec=pltpu.PrefetchScalarGridSpec(
            num_scalar_prefetch=1,
            grid=(dbatch, ng),
            in_specs=in_specs,
            out_specs=pl.BlockSpec((1, tdec, width), lambda b, g, pt: (b, 0, 0)),
            scratch_shapes=[pltpu.VMEM((rows, past), F32), pltpu.VMEM((nheads, past, dh), BF16)]),
        out_shape=jax.ShapeDtypeStruct((dbatch, tdec, width), F32),
        compiler_params=_cparams(("parallel", "arbitrary")),
        name="paged_" + mode,
    )(page_table_flat, q_rows_bf, *([k_pool] * gp), *([v_pool] * gp), *operands)


def _diff_decode_tables(rel_bias, n_pages, gp, tdec):
    rb = rel_bias.astype(F32)
    rel = (rb - rb[N_BUCKETS - 1][None, :]).T
    past = n_pages * PAGE
    t = np.arange(tdec)[:, None]
    kpos = past - gp * PAGE + np.arange(gp * PAGE)[None, :]
    assert PAGE + 1 >= _T5_FAR
    last = rel[:, _t5_bucket_np(past + t - kpos)]
    s = np.arange(tdec)[None, :]
    selfb = jnp.where(jnp.asarray(t >= s)[None], rel[:, _t5_bucket_np(t - s)], NEG)
    rep = lambda x: jnp.broadcast_to(x[:, None], (H_A, 2) + x.shape[1:]).reshape((H_A * 2 * tdec,) + x.shape[2:])
    last_r = rep(last)
    tab = jnp.stack([jnp.zeros_like(last_r), last_r], 0)
    return tab, rep(selfb)


def _retention_kernel(q_ref, k_ref, v_ref, gb_ref, cos_ref, sin_ref, qdec_ref, kdec_ref, dmat_ref,
                      gc_ref, bd_ref, s0_ref, o_ref, sout_ref, s_ref):
    c = pl.program_id(1)

    @pl.when(c == 0)
    def _():
        s_ref[...] = s0_ref[0]

    cos = cos_ref[...]
    sin = sin_ref[...]
    width = cos.shape[1]
    lane = lax.broadcasted_iota(jnp.int32, cos.shape, 1)
    first_half = (lane % DKB) < (DKB // 2)

    def rot(x):
        swapped = jnp.where(first_half, -pltpu.roll(x, width - DKB // 2, 1), pltpu.roll(x, DKB // 2, 1))
        return x * cos + swapped * sin

    qr = rot(q_ref[...])
    kr = rot(k_ref[...]) * (DKB ** -0.5)
    v = v_ref[...]
    vb = v.astype(BF16)
    state = s_ref[...]
    cross = jnp.dot((qr * qdec_ref[...]).astype(BF16), state.astype(BF16), preferred_element_type=F32)
    lane_pair = lax.broadcasted_iota(jnp.int32, (qr.shape[0], LANES), 1)
    gate = gb_ref[...]
    for h in range(H_B):
        cb = (h * DKB) // LANES
        lo = (h * DKB) % LANES
        qh = qr[:, cb * LANES:(cb + 1) * LANES]
        qh = jnp.where((lane_pair >= lo) & (lane_pair < lo + DKB), qh, 0.0).astype(BF16)
        kh = kr[:, cb * LANES:(cb + 1) * LANES].astype(BF16)
        att = _nt_dot(qh, kh) * dmat_ref[h]
        oh = jnp.dot(att.astype(BF16), vb[:, h * DVB:(h + 1) * DVB], preferred_element_type=F32)
        oh = oh + cross[:, h * DVB:(h + 1) * DVB]
        ms = jnp.mean(oh * oh, axis=-1, keepdims=True)
        gh = gate[:, h * DVB:(h + 1) * DVB]
        o_ref[:, h * DVB:(h + 1) * DVB] = oh * lax.rsqrt(ms + RMS_EPS) * (gh * _sigmoid(gh))
    upd = _tn_dot((kr * kdec_ref[...]).astype(BF16), vb)
    new_state = state * gc_ref[...] + upd * bd_ref[...]
    s_ref[...] = new_state

    @pl.when(c == pl.num_programs(1) - 1)
    def _():
        sout_ref[0] = new_state


def _retention(h0, row0, batch, seq, start, s0_bd):
    ch = RET_CHUNK if seq % RET_CHUNK == 0 else seq
    n = seq // ch
    assert row0 % ch == 0
    blk0 = row0 // ch
    kw = H_B * DKB
    vw = H_B * DVB
    qcol = (2 * H_A * DKA + H_A * DVA) // kw
    kcol = qcol + 1
    vcol = (2 * H_A * DKA + H_A * DVA + 2 * kw) // vw
    gcol = vcol + 1
    half = DKB // 2
    inv = 1.0 / (10000.0 ** (jnp.arange(half, dtype=F32) / half))
    pos = (start + jnp.arange(seq)).astype(F32)
    ang = pos[:, None] * inv[None, :]
    cos = jnp.tile(jnp.cos(ang), (1, 2 * H_B))
    sin = jnp.tile(jnp.sin(ang), (1, 2 * H_B))
    log_g = jnp.log(1.0 - jnp.exp2(-5.0 - jnp.arange(H_B, dtype=F32)))
    idx = jnp.arange(ch, dtype=F32)
    diff = idx[:, None] - idx[None, :]
    dmat = jnp.where(diff >= 0, jnp.exp(log_g[:, None, None] * jnp.maximum(diff, 0.0)), 0.0)
    qdec = jnp.repeat(jnp.exp(log_g[None, :] * (idx[:, None] + 1.0)), DKB, axis=1)
    kdec = jnp.repeat(jnp.exp(log_g[None, :] * (ch - 1.0 - idx[:, None])), DKB, axis=1)
    gc = jnp.broadcast_to(jnp.repeat(jnp.exp(log_g * ch), DKB)[:, None], (kw, vw))
    bd = jnp.asarray(np.kron(np.eye(H_B, dtype=np.float32), np.ones((DKB, DVB), np.float32)))
    out, s_fin = pl.pallas_call(
        _retention_kernel,
        grid=(batch, n),
        in_specs=[pl.BlockSpec((ch, kw), lambda b, c: (blk0 + b * n + c, qcol)),
                  pl.BlockSpec((ch, kw), lambda b, c: (blk0 + b * n + c, kcol)),
                  pl.BlockSpec((ch, vw), lambda b, c: (blk0 + b * n + c, vcol)),
                  pl.BlockSpec((ch, vw), lambda b, c: (blk0 + b * n + c, gcol)),
                  pl.BlockSpec((ch, kw), lambda b, c: (c, 0)),
                  pl.BlockSpec((ch, kw), lambda b, c: (c, 0)),
                  pl.BlockSpec((ch, kw), lambda b, c: (0, 0)),
                  pl.BlockSpec((ch, kw), lambda b, c: (0, 0)),
                  pl.BlockSpec((H_B, ch, ch), lambda b, c: (0, 0, 0)),
                  pl.BlockSpec((kw, vw), lambda b, c: (0, 0)),
                  pl.BlockSpec((kw, vw), lambda b, c: (0, 0)),
                  pl.BlockSpec((1, kw, vw), lambda b, c: (b, 0, 0))],
        out_specs=[pl.BlockSpec((ch, vw), lambda b, c: (b * n + c, 0)),
                   pl.BlockSpec((1, kw, vw), lambda b, c: (b, 0, 0))],
        out_shape=[jax.ShapeDtypeStruct((batch * seq, vw), F32),
                   jax.ShapeDtypeStruct((batch, kw, vw), F32)],
        scratch_shapes=[pltpu.VMEM((kw, vw), F32)],
        compiler_params=_cparams(("parallel", "arbitrary")),
        name="retention",
    )(h0, h0, h0, h0, cos, sin, qdec, kdec, dmat, gc, bd, s0_bd)
    return out, s_fin


def _state_to_bd(s):
    b = s.shape[0]
    eye = jnp.eye(H_B, dtype=s.dtype)
    return (s[:, :, :, None, :] * eye[None, :, None, :, None]).reshape(b, H_B * DKB, H_B * DVB)


def _bd_to_state(sbd):
    b = sbd.shape[0]
    s5 = sbd.reshape(b, H_B, DKB, H_B, DVB)
    return jnp.stack([s5[:, h, :, h, :] for h in range(H_B)], axis=1)


def _rwkv_prep_kernel(pc_ref, prev_ref, mu_ref, w0_ref, a0_ref, kkw_ref, kaw_ref, rk_ref,
                      w2_ref, a2_ref, g2_ref, ones_ref,
                      r_ref, w_ref, k_ref, v_ref, kk_ref, b_ref, g_ref, bonus_ref):
    pc = pc_ref[...]
    pm = pc + (prev_ref[...] - pc) * mu_ref[...]
    r = pm[:, 0:C_W]
    k = pm[:, C_W:2 * C_W]
    v = pm[:, 2 * C_W:3 * C_W]
    wa = pm[:, 3 * C_W:3 * C_W + W_LORA + A_LORA]
    gd = pm[:, 3 * C_W + W_LORA + A_LORA:]
    wl = jnp.dot(jnp.tanh(wa).astype(BF16), w2_ref[...], preferred_element_type=F32)
    w_log = -_softplus(-(w0_ref[...] + wl)) - 0.5
    decay = jnp.exp(-jnp.exp(w_log))
    a = _sigmoid(a0_ref[...] + jnp.dot(wa.astype(BF16), a2_ref[...], preferred_element_type=F32))
    g = jnp.dot(_sigmoid(gd).astype(BF16), g2_ref[...], preferred_element_type=F32)

    def seg_sum(x):
        return jnp.dot(x, ones_ref[...], precision=HIGHEST, preferred_element_type=F32)

    kkr = k * kkw_ref[...]
    kk = kkr / jnp.maximum(jnp.sqrt(seg_sum(kkr * kkr)), 1e-12)
    k2 = k * (1.0 + (a - 1.0) * kaw_ref[...])
    r_ref[...] = r
    w_ref[...] = decay
    k_ref[...] = k2
    v_ref[...] = v
    kk_ref[...] = kk
    b_ref[...] = kk * a
    g_ref[...] = g
    bonus_ref[...] = seg_sum(r * k2 * rk_ref[...]) * v


def _rwkv_prep(h1, prev, mu_p, w0, a0, kkw, kaw, rk, w2p, a2p, g2, ones_bd):
    m = h1.shape[0]
    tm = _pick_tile(m, 320)
    row = lambda x: x.reshape(1, -1)
    tok = lambda w: pl.BlockSpec((tm, w), lambda i: (i, 0))
    full = lambda a: pl.BlockSpec(a.shape, lambda i: (0, 0))
    small = [row(mu_p), row(w0), row(a0), row(kkw), row(kaw), row(rk), w2p, a2p, g2, ones_bd]
    outs = pl.pallas_call(
        _rwkv_prep_kernel,
        grid=(m // tm,),
        in_specs=[tok(C_PROJ), tok(C_PROJ)] + [full(a) for a in small],
        out_specs=[tok(C_W)] * 8,
        out_shape=[jax.ShapeDtypeStruct((m, C_W), F32)] * 8,
        compiler_params=_cparams(("parallel",)),
        name="rwkv_prep",
    )(h1, prev, *small)
    return outs


def _split_bf16(x):
    hi = x.astype(BF16)
    return hi, (x - hi.astype(F32)).astype(BF16)


def _dot3(a, b, dot, axis):
    ah, al = _split_bf16(a)
    bh, bl = _split_bf16(b)
    return dot(jnp.concatenate([ah, ah, al], axis=axis), jnp.concatenate([bh, bl, bh], axis=axis))


def _rwkv_scan_kernel(r_ref, w_ref, k_ref, kk_ref, b_ref, v_ref, s0_ref, o_ref, sout_ref, s_ref, *, tc):
    tb = pl.program_id(1)

    @pl.when(tb == 0)
    def _():
        s_ref[...] = s0_ref[...]

    nb, npair = s_ref.shape[0], s_ref.shape[1]
    pairs = [(bi, p) for bi in range(nb) for p in range(npair)]
    g = len(pairs)
    pw = 2 * NC
    shape = (g, NC, pw)
    lo = lax.broadcasted_iota(jnp.int32, shape, 2) < NC
    lo_rows = lax.broadcasted_iota(jnp.int32, (tc, pw), 1) < NC
    step_of_row = lax.broadcasted_iota(jnp.int32, (2 * tc, pw), 0) % tc

    def rows(ref, t):
        return jnp.concatenate([ref[bi, t:t + 1, p * pw:(p + 1) * pw][None] for bi, p in pairs], axis=0)

    def half_sums(x):
        even = jnp.sum(jnp.where(lo, x, 0.0), axis=2, keepdims=True)
        odd = jnp.sum(jnp.where(lo, 0.0, x), axis=2, keepdims=True)
        return even, odd

    def parity_rows(x):
        return jnp.concatenate([jnp.where(lo_rows, x, 0.0), jnp.where(lo_rows, 0.0, x)], axis=0)

    vk, rm = [], []
    for bi, p in pairs:
        sl = slice(p * pw, (p + 1) * pw)
        kp = parity_rows(k_ref[bi, :, sl])
        kb = jnp.concatenate([jnp.where(step_of_row == t, kp, 0.0) for t in range(tc)], axis=1)
        v8 = v_ref[bi, :, sl]
        va = jnp.concatenate([v8[:, 0:NC], v8[:, NC:pw]], axis=0)
        vk.append(_dot3(va, kb, _tn_dot, 0))
        rm.append(parity_rows(r_ref[bi, :, sl]))

    state = s_ref[...].reshape(shape)
    acc = [jnp.zeros((2 * tc, NC), F32) for _ in pairs]
    for t in range(tc):
        sa_e, sa_o = half_sums(state * rows(kk_ref, t))
        sa = jnp.where(lo, sa_e, sa_o)
        vk_t = jnp.concatenate([x[:, t * pw:(t + 1) * pw][None] for x in vk], axis=0)
        state = state * rows(w_ref, t) - sa * rows(b_ref, t) + vk_t
        for q in range(g):
            acc[q] = acc[q] + _dot3(jnp.where(step_of_row == t, rm[q], 0.0), state[q], _nt_dot, 1)
    s_ref[...] = state.reshape(s_ref.shape)
    for q, (bi, p) in enumerate(pairs):
        o_ref[bi, :, p * pw:(p + 1) * pw] = jnp.concatenate([acc[q][0:tc], acc[q][tc:2 * tc]], axis=1)

    @pl.when(tb == pl.num_programs(1) - 1)
    def _():
        sout_ref[...] = s_ref[...]


def _rwkv_scan(r, w, k, kk, b, v, s0, batch, seq):
    tc = SCAN_TC
    assert seq % tc == 0 and batch % 2 == 0
    nt = seq // tc
    npair = H_C // 2
    rows3 = lambda x: x.reshape(batch, seq, C_W)
    s0p = s0.reshape(batch, npair, 2, NC, NC).transpose(0, 1, 3, 2, 4).reshape(batch, npair, NC, 2 * NC)
    row_spec = pl.BlockSpec((2, tc, C_W), lambda bb, tb: (bb, tb, 0))
    st_spec = pl.BlockSpec((2, npair, NC, 2 * NC), lambda bb, tb: (bb, 0, 0, 0))
    o, s_fin = pl.pallas_call(
        functools.partial(_rwkv_scan_kernel, tc=tc),
        grid=(batch // 2, nt),
        in_specs=[row_spec] * 6 + [st_spec],
        out_specs=[row_spec, st_spec],
        out_shape=[jax.ShapeDtypeStruct((batch, seq, C_W), F32),
                   jax.ShapeDtypeStruct((batch, npair, NC, 2 * NC), F32)],
        scratch_shapes=[pltpu.VMEM((2, npair, NC, 2 * NC), F32)],
        compiler_params=_cparams(("parallel", "arbitrary")),
        name="rwkv_scan",
    )(rows3(r), rows3(w), rows3(k), rows3(kk), rows3(b), rows3(v), s0p)
    s_new = s_fin.reshape(batch, npair, NC, 2, NC).transpose(0, 1, 3, 2, 4).reshape(batch, H_C, NC, NC)
    return o.reshape(batch * seq, C_W), s_new


def _rwkv_post_kernel(o_ref, bonus_ref, g_ref, lng_ref, lnb_ref, ones_ref, out_ref):
    def seg_mean(x):
        return jnp.dot(x, ones_ref[...], precision=HIGHEST, preferred_element_type=F32) * (1.0 / NC)

    o = o_ref[...]
    oc = o - seg_mean(o)
    on = oc * lax.rsqrt(seg_mean(oc * oc) + GN_EPS_C)
    out_ref[...] = (on * lng_ref[...] + lnb_ref[...] + bonus_ref[...]) * g_ref[...]


def _rwkv_post(o, bonus, g, ln_g, ln_b, ones_bd):
    m = o.shape[0]
    tm = _pick_tile(m, 640)
    tok = pl.BlockSpec((tm, C_W), lambda i: (i, 0))
    vec = pl.BlockSpec((1, C_W), lambda i: (0, 0))
    return pl.pallas_call(
        _rwkv_post_kernel,
        grid=(m // tm,),
        in_specs=[tok, tok, tok, vec, vec, pl.BlockSpec((C_W, C_W), lambda i: (0, 0))],
        out_specs=tok,
        out_shape=jax.ShapeDtypeStruct((m, C_W), F32),
        compiler_params=_cparams(("parallel",)),
        name="rwkv_post",
    )(o, bonus, g, ln_g.reshape(1, C_W), ln_b.reshape(1, C_W), ones_bd)


def _perm_odd_columns():
    offs = np.cumsum([0, C_W, W_LORA, C_W, C_W, A_LORA, G_LORA])
    seg = lambda i: np.arange(offs[i], offs[i + 1])
    pc_perm = np.concatenate([seg(0), seg(2), seg(3), seg(1), seg(4), seg(5)])
    return pc_perm


def kernel(x_prompt, x_sample, cache_a_k, cache_a_v, state_b, state_c_wkv, state_c_shift, cache_d_k, cache_d_v,
           cache_d_logf, page_table, rel_bias, w_in_even, lam_params, subln_g, w_out_even, w_in_odd, c_mu, c_w0,
           c_w2, c_a0, c_a2, c_g2, c_kk, c_ka, c_rk, c_ln_g, c_ln_b, d_bf, w_out_odd, w_router, b_router, w_gate,
           w_up, w_down, ln1_g, ln1_b, ln2_g, ln2_b):
    batch, seq, d = x_prompt.shape
    dbatch, tdec, _ = x_sample.shape
    n_pages = page_table.shape[1]
    n_pool = cache_a_k.shape[1]
    past = n_pages * PAGE
    bt = batch * seq
    ds = dbatch * tdec
    pt_flat = page_table.reshape(-1).astype(jnp.int32)
    gp = _pick_tile(n_pages, PAGES_PER_STEP, 1)

    x = jnp.concatenate([x_prompt.reshape(bt, d), x_sample.reshape(ds, d)], axis=0)
    outs = {}

    def moe_block(x_in, layer):
        gates = _router(x_in, w_router, b_router)
        wgu = jnp.concatenate([w_gate[layer], w_up[layer]], axis=-1).astype(BF16)
        return _moe_ln(x_in, gates, wgu, w_down[layer].astype(BF16), ln2_g[layer], ln2_b[layer])

    lam_init0 = 0.8 - 0.6 * math.exp(-0.3 * 0)
    h0 = _proj(x, w_in_even[0].astype(BF16))
    ka_off = H_A * DKA
    va_off = 2 * H_A * DKA
    ka_all = h0[:, ka_off:ka_off + H_A * DKA]
    va_all = h0[:, va_off:va_off + H_A * DVA]
    outs["a_k_p"] = ka_all[:bt].reshape(1, batch, seq, H_A, DKA)
    outs["a_v_p"] = va_all[:bt].reshape(1, batch, seq, H_A, DVA)
    outs["a_k_s"] = ka_all[bt:].reshape(1, dbatch, tdec, H_A, DKA)
    outs["a_v_s"] = va_all[bt:].reshape(1, dbatch, tdec, H_A, DVA)

    oa_p = _diff_prompt(h0, batch, seq, rel_bias, lam_params[0], subln_g[0], lam_init0)
    qa_s = h0[bt:, 0:H_A * DKA].reshape(dbatch, tdec, H_A, DKA) * (DA ** -0.5)
    q_rows_a = _query_rows(qa_s, 2).astype(BF16)
    tab_d, tself_d = _diff_decode_tables(rel_bias, n_pages, gp, tdec)
    oa_s = _paged_attention(
        "diff", q_rows_a, cache_a_k.reshape(n_pool, PAGE * H_A, DKA), cache_a_v.reshape(n_pool, PAGE * H_A, DVA),
        pt_flat, n_pages, ka_all[bt:].reshape(dbatch, tdec, H_A * DKA), va_all[bt:].reshape(dbatch, tdec, H_A * DVA),
        (tab_d, tself_d, lam_params[0], subln_g[0].reshape(1, DVA)), 1.0, lam_init0)
    ob_p, sb_p = _retention(h0, 0, batch, seq, 0, jnp.zeros((batch, H_B * DKB, H_B * DVB), F32))
    ob_s, sb_s = _retention(h0, bt, dbatch, tdec, past, _state_to_bd(state_b[0].astype(F32)))
    outs["b_p"] = _bd_to_state(sb_p)[None]
    outs["b_s"] = _bd_to_state(sb_s)[None]
    mix_l = jnp.concatenate([oa_p, oa_s.reshape(ds, H_A * DVA)], axis=0)
    mix_r = jnp.concatenate([ob_p, ob_s], axis=0)
    x = _outproj_ln(mix_l, mix_r, x, w_out_even[0].astype(BF16), ln1_g[0], ln1_b[0])
    x = moe_block(x, 0)

    pc_perm = _perm_odd_columns()
    w_odd = w_in_odd[0]
    w_odd_p = jnp.concatenate([w_odd[:, pc_perm], w_odd[:, C_PROJ:],
                               jnp.zeros((d, ODD_IN_PAD - w_odd.shape[1]), w_odd.dtype)], axis=1).astype(BF16)
    h1 = _proj(x, w_odd_p)
    q_off = C_PROJ
    k_off = C_PROJ + D_W
    v_off = C_PROJ + 2 * D_W
    kd_all = h1[:, k_off:k_off + D_W]
    vd_all = h1[:, v_off:v_off + D_W]
    outs["d_k_p"] = kd_all[:bt].reshape(1, batch, seq, H_D, DD)
    outs["d_v_p"] = vd_all[:bt].reshape(1, batch, seq, H_D, DD)
    outs["d_k_s"] = kd_all[bt:].reshape(1, dbatch, tdec, H_D, DD)
    outs["d_v_s"] = vd_all[bt:].reshape(1, dbatch, tdec, H_D, DD)

    bf_pad = jnp.zeros((1, LANES), F32).at[0, :H_D].set(d_bf[0].astype(F32))
    lf_p, c_p = _fox_prep(h1, bf_pad, 0, batch, seq)
    lf_s, c_s = _fox_prep(h1, bf_pad, bt, dbatch, tdec)
    outs["d_lf_p"] = lf_p[:, :H_D].reshape(1, batch, seq, H_D)
    outs["d_lf_s"] = lf_s[:, :H_D].reshape(1, dbatch, tdec, H_D)
    od_p = _fox_prompt(h1, c_p, batch, seq)
    cn = c_s[:, :H_D].reshape(dbatch, tdec, H_D)
    cnt = cn.transpose(0, 2, 1)
    rho = cnt.reshape(dbatch, H_D * tdec, 1)
    logf_pool_t = cache_d_logf[0].astype(F32).transpose(0, 2, 1)
    kap = _suffix_sums(logf_pool_t, pt_flat, dbatch, n_pages)
    qd_s = h1[bt:, q_off:q_off + D_W].reshape(dbatch, tdec, H_D, DD)
    q_rows_d = _query_rows(qd_s, 1).astype(BF16)
    od_s = _paged_attention(
        "fox", q_rows_d, cache_d_k.reshape(n_pool, PAGE * H_D, DD), cache_d_v.reshape(n_pool, PAGE * H_D, DD),
        pt_flat, n_pages, kd_all[bt:].reshape(dbatch, tdec, D_W), vd_all[bt:].reshape(dbatch, tdec, D_W),
        (kap, rho, cnt), DD ** -0.5)

    pc_p = h1[:bt, :C_PROJ].reshape(batch, seq, C_PROJ)
    pc_s = h1[bt:, :C_PROJ].reshape(dbatch, tdec, C_PROJ)
    shift_p = jnp.zeros((batch, 1, C_PROJ), F32)
    shift_s = state_c_shift[0][:, pc_perm][:, None, :].astype(F32)
    prev = jnp.concatenate([
        jnp.concatenate([shift_p, pc_p[:, :-1]], axis=1).reshape(bt, C_PROJ),
        jnp.concatenate([shift_s, pc_s[:, :-1]], axis=1).reshape(ds, C_PROJ)], axis=0)
    inv_perm = np.argsort(pc_perm)
    outs["c_shift_p"] = pc_p[:, -1][:, inv_perm][None]
    outs["c_shift_s"] = pc_s[:, -1][:, inv_perm][None]
    w2p = jnp.concatenate([c_w2[0], jnp.zeros((A_LORA, C_W), F32)], axis=0).astype(BF16)
    a2p = jnp.concatenate([jnp.zeros((W_LORA, C_W), F32), c_a2[0]], axis=0).astype(BF16)
    ones_bd = jnp.asarray(np.kron(np.eye(H_C, dtype=np.float32), np.ones((NC, NC), np.float32)))
    r_, w_, k_, v_, kk_, b_, g_, bonus = _rwkv_prep(
        h1, prev, c_mu[0][pc_perm], c_w0[0], c_a0[0], c_kk[0], c_ka[0], c_rk[0].reshape(-1),
        w2p, a2p, c_g2[0].astype(BF16), ones_bd)
    o_p, sc_p = _rwkv_scan(r_[:bt], w_[:bt], k_[:bt], kk_[:bt], b_[:bt], v_[:bt],
                           jnp.zeros((batch, H_C, NC, NC), F32), batch, seq)
    o_s, sc_s = _rwkv_scan(r_[bt:], w_[bt:], k_[bt:], kk_[bt:], b_[bt:], v_[bt:],
                           state_c_wkv[0].astype(F32), dbatch, tdec)
    outs["c_wkv_p"] = sc_p[None]
    outs["c_wkv_s"] = sc_s[None]
    oc = _rwkv_post(jnp.concatenate([o_p, o_s], axis=0), bonus, g_, c_ln_g[0], c_ln_b[0], ones_bd)
    mix_r = jnp.concatenate([od_p, od_s.reshape(ds, D_W)], axis=0)
    x = _outproj_ln(oc, mix_r, x, w_out_odd[0].astype(BF16), ln1_g[1], ln1_b[1])
    x = moe_block(x, 1)

    y_prompt = x[:bt].reshape(batch, seq, d)
    y_sample = x[bt:].reshape(dbatch, tdec, d)
    return (y_prompt, y_sample,
            outs["a_k_p"], outs["a_v_p"], outs["a_k_s"], outs["a_v_s"],
            outs["b_p"], outs["b_s"],
            outs["c_wkv_p"], outs["c_wkv_s"], outs["c_shift_p"], outs["c_shift_s"],
            outs["d_k_p"], outs["d_v_p"], outs["d_lf_p"], outs["d_k_s"], outs["d_v_s"], outs["d_lf_s"])
```
